```python
import math
import jax, jax.numpy as jnp
from jax import lax
import numpy as np

D_MODEL = 1024
BATCH = 8
SEQ = 4096
DEPTH = 1

ATTN_HEADS = 16
ATTN_HEAD_DIM = 64
ATTN_WIDTH = ATTN_HEADS * ATTN_HEAD_DIM
DILATION_PATTERNS = ((128, 1), (512, 4), (2048, 16))
GLA_HEADS = 4
GLA_KEY_WIDTH = D_MODEL // 2
GLA_VALUE_WIDTH = D_MODEL
GLA_KEY_DIM = GLA_KEY_WIDTH // GLA_HEADS
GLA_VALUE_DIM = GLA_VALUE_WIDTH // GLA_HEADS
GLA_GATE_RANK = 16
GLA_GATE_TAU = 16.0
GLA_CHUNK = 64
N_BRANCHES = 2
NORM_EPS = 1e-6
GROUP_NORM_EPS = 1e-5

IN_SPLITS = (
    ATTN_WIDTH,
    ATTN_WIDTH,
    ATTN_WIDTH,
    ATTN_WIDTH,
    GLA_KEY_WIDTH,
    GLA_KEY_WIDTH,
    GLA_VALUE_WIDTH,
    GLA_VALUE_WIDTH,
    GLA_GATE_RANK,
    D_MODEL,
    D_MODEL,
)
IN_COLS = int(sum(IN_SPLITS))
IN_OFFSETS = tuple(int(o) for o in np.cumsum(IN_SPLITS)[:-1])

kernel_name = 'hybrid_dilated_attn_gla_gated_block'


def rmsnorm(x, gain):
    xf = x.astype(jnp.float32)
    y = xf * lax.rsqrt(jnp.mean(xf * xf, axis=-1, keepdims=True) + NORM_EPS) * gain.astype(jnp.float32)
    return y.astype(x.dtype)


def alibi_slopes(n_heads):
    return 2.0 ** (-8.0 * (jnp.arange(n_heads, dtype=jnp.float32) + 1.0) / n_heads)


def dilated_attention(q, k, v, window, dilation, slopes):
    bsz, seq, heads, hd = q.shape
    n = window // dilation
    unit = n * dilation
    seq_pad = -(-seq // unit) * unit
    length = seq_pad // dilation
    nb = length // n

    def strided(t):
        t = jnp.pad(t, ((0, 0), (0, seq_pad - seq), (0, 0), (0, 0)))
        t = t.reshape(bsz, length, dilation, heads, hd).transpose(0, 2, 1, 3, 4)
        return t.reshape(bsz, dilation, nb, n, heads, hd)

    def with_prev(t):
        prev = jnp.pad(t, ((0, 0), (0, 0), (1, 0), (0, 0), (0, 0), (0, 0)))[:, :, :-1]
        return jnp.concatenate([prev, t], axis=3)

    qs = strided(q)
    kb = with_prev(strided(k))
    vb = with_prev(strided(v))

    scores = jnp.einsum('brnqhd,brnkhd->brnhqk', qs, kb).astype(jnp.float32) * (hd ** -0.5)
    qi = jnp.arange(n)[:, None]
    kj = jnp.arange(2 * n)[None, :]
    steps = n + qi - kj
    band = (steps >= 0) & (steps <= n)
    first = (jnp.arange(nb)[:, None, None] == 0) & (kj < n)[None]
    valid = band[None] & ~first
    bias = -slopes[:, None, None] * (steps * dilation).astype(jnp.float32)[None]
    s = jnp.where(valid[:, None], scores + bias, -jnp.inf)
    m = jnp.max(s, axis=-1, keepdims=True)
    p = jnp.exp(s - m)
    den = jnp.sum(p, axis=-1, keepdims=True)
    o = jnp.einsum('brnhqk,brnkhd->brnqhd', p / den, vb.astype(jnp.float32))
    lse = (m + jnp.log(den))[..., 0]

    o = o.reshape(bsz, dilation, length, heads, hd).transpose(0, 2, 1, 3, 4)
    o = o.reshape(bsz, seq_pad, heads, hd)[:, :seq]
    lse = lse.transpose(0, 1, 2, 4, 3).reshape(bsz, dilation, length, heads)
    lse = lse.transpose(0, 2, 1, 3).reshape(bsz, seq_pad, heads)[:, :seq]
    return o, lse


def dilated_mixture_attention(q, k, v):
    slopes = alibi_slopes(q.shape[2])
    outs, lses = [], []
    for window, dilation in DILATION_PATTERNS:
        o, lse = dilated_attention(q, k, v, window, dilation, slopes)
        outs.append(o)
        lses.append(lse)
    w = jax.nn.softmax(jnp.stack(lses, axis=0), axis=0)
    return jnp.sum(w[..., None] * jnp.stack(outs, axis=0), axis=0)


def gla_chunked(q, k, v, log_a):
    bsz, seq, heads, dk = q.shape
    dv = v.shape[-1]
    c = GLA_CHUNK
    nc = seq // c
    q = q.astype(jnp.float32).reshape(bsz, nc, c, heads, dk) * (dk ** -0.5)
    k = k.astype(jnp.float32).reshape(bsz, nc, c, heads, dk)
    v = v.astype(jnp.float32).reshape(bsz, nc, c, heads, dv)
    b = jnp.cumsum(log_a.astype(jnp.float32).reshape(bsz, nc, c, heads, dk), axis=2)
    q_dec = q * jnp.exp(b)
    k_inv = k * jnp.exp(-b)
    causal = jnp.tril(jnp.ones((c, c), dtype=bool))
    a_intra = jnp.where(causal, jnp.einsum('bnihd,bnjhd->bnhij', q_dec, k_inv), 0.0)
    o_intra = jnp.einsum('bnhij,bnjhe->bnihe', a_intra, v)
    b_last = b[:, :, -1]
    k_to_end = k * jnp.exp(b_last[:, :, None] - b)
    d_state = jnp.einsum('bnjhd,bnjhe->bnhde', k_to_end, v)

    def step(state, xs):
        q_c, decay_c, ds_c = xs
        o_c = jnp.einsum('bihd,bhde->bihe', q_c, state)
        return decay_c[..., None] * state + ds_c, o_c

    state0 = jnp.zeros((bsz, heads, dk, dv), jnp.float32)
    _, o_inter = lax.scan(step, state0, (jnp.moveaxis(q_dec, 1, 0),
                                         jnp.moveaxis(jnp.exp(b_last), 1, 0),
                                         jnp.moveaxis(d_state, 1, 0)))
    o = o_intra + jnp.moveaxis(o_inter, 0, 1)
    return o.reshape(bsz, seq, heads, dv)


def head_group_norm(o, gain):
    mu = jnp.mean(o, axis=-1, keepdims=True)
    var = jnp.mean(jnp.square(o - mu), axis=-1, keepdims=True)
    y = (o - mu) * lax.rsqrt(var + GROUP_NORM_EPS)
    return y * gain.astype(jnp.float32).reshape(o.shape[-2], o.shape[-1])


def setup_inputs(seed: int = 0) -> dict:
    key = jax.random.key(seed)
    ks = jax.random.split(key, 12)
    f = jnp.float32
    return {
        'x': jax.random.normal(ks[0], (BATCH, SEQ, D_MODEL), f),
        'norm_gain': 1.0 + 0.02 * jax.random.normal(ks[1], (DEPTH, D_MODEL), f),
        'w_in': jax.random.normal(ks[2], (DEPTH, D_MODEL, IN_COLS), f) * D_MODEL ** -0.5,
        'b_gate': 0.01 * jax.random.normal(ks[3], (DEPTH, N_BRANCHES * D_MODEL), f),
        'w_alpha': jax.random.normal(ks[4], (DEPTH, GLA_GATE_RANK, GLA_KEY_WIDTH), f) * GLA_GATE_RANK ** -0.5,
        'b_alpha': 0.1 * jax.random.normal(ks[5], (DEPTH, GLA_KEY_WIDTH), f),
        'gla_norm_gain': 1.0 + 0.02 * jax.random.normal(ks[6], (DEPTH, GLA_VALUE_WIDTH), f),
        'w_out_attn': jax.random.normal(ks[7], (DEPTH, ATTN_WIDTH, D_MODEL), f) * ATTN_WIDTH ** -0.5,
        'w_out_gla': jax.random.normal(ks[8], (DEPTH, GLA_VALUE_WIDTH, D_MODEL), f) * GLA_VALUE_WIDTH ** -0.5,
        'w_out': jax.random.normal(ks[9], (DEPTH, D_MODEL, D_MODEL), f) * D_MODEL ** -0.5,
        'final_norm_gain': 1.0 + 0.02 * jax.random.normal(ks[10], (D_MODEL,), f),
    }


def reference(x, norm_gain, w_in, b_gate, w_alpha, b_alpha, gla_norm_gain,
              w_out_attn, w_out_gla, w_out, final_norm_gain):
    bsz, seq, _ = x.shape
    h = x
    for layer in range(DEPTH):
        u = rmsnorm(h, norm_gain[layer])
        proj = u @ w_in[layer]
        (q_a, k_a, v_a, z_a, q_b, k_b, v_b, z_b, a_code, g_a, g_b) = jnp.split(proj, IN_OFFSETS, axis=-1)

        o_a = dilated_mixture_attention(q_a.reshape(bsz, seq, ATTN_HEADS, ATTN_HEAD_DIM),
                                        k_a.reshape(bsz, seq, ATTN_HEADS, ATTN_HEAD_DIM),
                                        v_a.reshape(bsz, seq, ATTN_HEADS, ATTN_HEAD_DIM))
        o_a = o_a.reshape(bsz, seq, ATTN_WIDTH).astype(x.dtype) * jax.nn.silu(z_a)
        y_a = o_a @ w_out_attn[layer]

        log_a = jax.nn.log_sigmoid((a_code @ w_alpha[layer] + b_alpha[layer]).astype(jnp.float32)) / GLA_GATE_TAU
        o_b = gla_chunked(q_b.reshape(bsz, seq, GLA_HEADS, GLA_KEY_DIM),
                          k_b.reshape(bsz, seq, GLA_HEADS, GLA_KEY_DIM),
                          v_b.reshape(bsz, seq, GLA_HEADS, GLA_VALUE_DIM),
                          log_a.reshape(bsz, seq, GLA_HEADS, GLA_KEY_DIM))
        o_b = head_group_norm(o_b, gla_norm_gain[layer]).reshape(bsz, seq, GLA_VALUE_WIDTH)
        o_b = o_b.astype(x.dtype) * jax.nn.silu(z_b)
        y_b = o_b @ w_out_gla[layer]

        gate_bias_a, gate_bias_b = jnp.split(b_gate[layer], N_BRANCHES)
        merged = jax.nn.sigmoid(g_a + gate_bias_a) * y_a + jax.nn.sigmoid(g_b + gate_bias_b) * y_b
        h = h + merged @ w_out[layer]
    return rmsnorm(h, final_norm_gain)
```

```python
import functools
import math

import jax
import jax.numpy as jnp
from jax import lax
from jax.experimental import pallas as pl
from jax.experimental.pallas import tpu as pltpu

ATTN_HEADS = 16
ATTN_HEAD_DIM = 64
ATTN_WINDOW_STEPS = 128
DILATIONS = (1, 4, 16)
GLA_HEADS = 4
GLA_GATE_RANK = 16
GLA_GATE_TAU = 16.0
GLA_CHUNK = 64
NORM_EPS = 1e-6
GROUP_NORM_EPS = 1e-5

LANES = 128
VMEM_LIMIT_BYTES = 56 * 1024 * 1024

MASK_VALUE = -1e30

F32 = jnp.float32
BF16 = jnp.bfloat16


def _dot(a, b):
    return jnp.dot(a, b, preferred_element_type=F32)


def _dot_nt(a, b):
    return lax.dot_general(a, b, (((1,), (1,)), ((), ())), preferred_element_type=F32)


def _dot_tn(a, b):
    return lax.dot_general(a, b, (((0,), (0,)), ((), ())), preferred_element_type=F32)


def _sigmoid(x):
    return 1.0 / (1.0 + jnp.exp(-x))


def _in_proj_kernel(x_ref, gain_ref, w_ref, wcode_ref, proj_ref, code_ref, u_ref):
    j = pl.program_id(1)

    @pl.when(j == 0)
    def _():
        x = x_ref[...]
        ms = jnp.mean(x * x, axis=-1, keepdims=True)
        u = (x * lax.rsqrt(ms + NORM_EPS) * gain_ref[...]).astype(BF16)
        u_ref[...] = u
        code_ref[...] = _dot(u, wcode_ref[...])

    proj_ref[...] = _dot(u_ref[...], w_ref[...]).astype(BF16)


def _in_proj(x2, gain, w_main, w_code, *, tm, tn):
    t, d = x2.shape
    n = w_main.shape[1]
    return pl.pallas_call(
        _in_proj_kernel,
        grid=(t // tm, n // tn),
        in_specs=[
            pl.BlockSpec((tm, d), lambda i, j: (i, 0)),
            pl.BlockSpec((1, d), lambda i, j: (0, 0)),
            pl.BlockSpec((d, tn), lambda i, j: (0, j)),
            pl.BlockSpec((d, LANES), lambda i, j: (0, 0)),
        ],
        out_specs=[
            pl.BlockSpec((tm, tn), lambda i, j: (i, j)),
            pl.BlockSpec((tm, LANES), lambda i, j: (i, 0)),
        ],
        out_shape=[
            jax.ShapeDtypeStruct((t, n), BF16),
            jax.ShapeDtypeStruct((t, LANES), F32),
        ],
        scratch_shapes=[pltpu.VMEM((tm, d), BF16)],
        compiler_params=pltpu.CompilerParams(
            dimension_semantics=("parallel", "arbitrary"),
            vmem_limit_bytes=VMEM_LIMIT_BYTES),
        name="in_proj",
    )(x2, gain, w_main, w_code)


def _attn_kernel(slopes_ref, q_ref, k_ref, v_ref, o_ref, lse_ref, *, dilation, pairs, n_qblocks):
    n = ATTN_WINDOW_STEPS
    step = pl.program_id(2)

    @pl.when(step == 0)
    def _():
        lse_ref[...] = jnp.zeros_like(lse_ref)

    row = lax.broadcasted_iota(jnp.int32, (2 * n, 2 * n), 0)
    col = lax.broadcasted_iota(jnp.int32, (2 * n, 2 * n), 1)
    steps = n + (row % n) - col
    band = (steps >= 0) & (steps <= n)
    dist = (steps * dilation).astype(F32)
    lane_q = lax.broadcasted_iota(jnp.int32, (n, LANES), 1)
    low_half = lane_q < ATTN_HEAD_DIM

    for p in range(pairs):
        pair = step * pairs + p
        slope = jnp.where(row < n, slopes_ref[2 * pair], slopes_ref[2 * pair + 1])
        bias = jnp.where(band, -slope * dist, MASK_VALUE)
        bias_first = jnp.where(col < n, MASK_VALUE, bias)
        lanes = slice(p * LANES, (p + 1) * LANES)

        def qblock(i, carry, lanes=lanes, bias=bias, bias_first=bias_first, pair=pair):
            r0 = pl.multiple_of(i * n, n)
            rp = pl.multiple_of(jnp.maximum(i - 1, 0) * n, n)
            q = q_ref[0, pl.ds(r0, n), lanes] * (ATTN_HEAD_DIM ** -0.5)
            zero = jnp.zeros_like(q)
            qq = jnp.concatenate([jnp.where(low_half, q, zero), jnp.where(low_half, zero, q)], axis=0)
            kk = jnp.concatenate([k_ref[0, pl.ds(rp, n), lanes], k_ref[0, pl.ds(r0, n), lanes]], axis=0)
            vv = jnp.concatenate([v_ref[0, pl.ds(rp, n), lanes], v_ref[0, pl.ds(r0, n), lanes]], axis=0)
            s = _dot_nt(qq, kk) + jnp.where(i == 0, bias_first, bias)
            m = jnp.max(s, axis=-1, keepdims=True)
            e = jnp.exp(s - m)
            den = jnp.sum(e, axis=-1, keepdims=True)
            pv = _dot(e.astype(BF16), vv) / den
            o_ref[0, pl.ds(r0, n), lanes] = jnp.where(low_half, pv[:n], pv[n:]).astype(o_ref.dtype)
            lse = m + jnp.log(den)
            cur = lse_ref[0, pl.ds(r0, n), :]
            cur = jnp.where(lane_q == 2 * pair, lse[:n], cur)
            cur = jnp.where(lane_q == 2 * pair + 1, lse[n:], cur)
            lse_ref[0, pl.ds(r0, n), :] = cur
            return carry

        lax.fori_loop(0, n_qblocks, qblock, 0)


def _attention_pattern(proj3, slopes, *, dilation, pairs):
    bsz, seq, ncols = proj3.shape
    width = ATTN_HEADS * ATTN_HEAD_DIM
    length = seq // dilation
    assert seq % (ATTN_WINDOW_STEPS * dilation) == 0
    bw = LANES * pairs
    cb = ncols // bw
    kb = width // bw
    view = proj3.reshape(bsz, length, dilation * ncols)

    def in_map(section):
        return lambda b, r, s, *_: (b, 0, r * cb + section * kb + s)

    kernel = functools.partial(_attn_kernel, dilation=dilation, pairs=pairs,
                               n_qblocks=length // ATTN_WINDOW_STEPS)
    o, lse = pl.pallas_call(
        kernel,
        grid_spec=pltpu.PrefetchScalarGridSpec(
            num_scalar_prefetch=1,
            grid=(bsz, dilation, kb),
            in_specs=[
                pl.BlockSpec((1, length, bw), in_map(0)),
                pl.BlockSpec((1, length, bw), in_map(1)),
                pl.BlockSpec((1, length, bw), in_map(2)),
            ],
            out_specs=[
                pl.BlockSpec((1, length, bw), lambda b, r, s, *_: (b, 0, r * kb + s)),
                pl.BlockSpec((1, length, LANES), lambda b, r, s, *_: (b, 0, r)),
            ],
        ),
        out_shape=[
            jax.ShapeDtypeStruct((bsz, length, dilation * width), BF16),
            jax.ShapeDtypeStruct((bsz, length, dilation * LANES), F32),
        ],
        compiler_params=pltpu.CompilerParams(
            dimension_semantics=("parallel", "parallel", "arbitrary"),
            vmem_limit_bytes=VMEM_LIMIT_BYTES),
        name=f"dilated_attn_d{dilation}",
    )(slopes, view, view, view)
    return o.reshape(bsz * seq, width), lse.reshape(bsz * seq, LANES)


def _split_bf16(x):
    hi = x.astype(BF16)
    return hi, (x - hi.astype(F32)).astype(BF16)


def _gla_kernel(q_ref, k_ref, v_ref, z_ref, code_ref, walpha_ref, balpha_ref, gain_ref,
                o_ref, state_ref, loga_ref, *, n_chunks):
    c = GLA_CHUNK
    dk = q_ref.shape[-1]

    code_hi, code_lo = _split_bf16(code_ref[0])
    w_hi, w_lo = _split_bf16(walpha_ref[...])
    logits = _dot(code_hi, w_hi) + _dot(code_hi, w_lo) + _dot(code_lo, w_hi) + balpha_ref[...]
    log_sig = jnp.minimum(logits, 0.0) - jnp.log1p(jnp.exp(-jnp.abs(logits)))
    loga_ref[...] = log_sig / GLA_GATE_TAU

    state_ref[...] = jnp.zeros_like(state_ref)
    ri = lax.broadcasted_iota(jnp.int32, (c, c), 0)
    ci = lax.broadcasted_iota(jnp.int32, (c, c), 1)
    causal = ri >= ci
    tri = causal.astype(BF16)
    gain = gain_ref[...]

    def chunk(ic, carry):
        r0 = pl.multiple_of(ic * c, c)
        la = loga_ref[pl.ds(r0, c), :]
        la1 = la.astype(BF16)
        rem = la - la1.astype(F32)
        la2 = rem.astype(BF16)
        la3 = (rem - la2.astype(F32)).astype(BF16)
        b = _dot(tri, la1) + _dot(tri, la2) + _dot(tri, la3)
        b_last = b[c - 1:c, :]
        q = q_ref[0, pl.ds(r0, c), :].astype(F32) * (dk ** -0.5)
        k = k_ref[0, pl.ds(r0, c), :].astype(F32)
        v = v_ref[0, pl.ds(r0, c), :]
        q_dec = (q * jnp.exp(b)).astype(BF16)
        k_inv = (k * jnp.exp(-b)).astype(BF16)
        k_end = (k * jnp.exp(b_last - b)).astype(BF16)
        a = jnp.where(causal, _dot_nt(q_dec, k_inv), 0.0)
        st = state_ref[...]
        o = _dot(a.astype(BF16), v) + _dot_nt(q_dec, st.astype(BF16))
        state_ref[...] = jnp.exp(b_last) * st + _dot_tn(v, k_end)
        mu = jnp.mean(o, axis=-1, keepdims=True)
        cen = o - mu
        var = jnp.mean(cen * cen, axis=-1, keepdims=True)
        y = cen * lax.rsqrt(var + GROUP_NORM_EPS) * gain
        z = z_ref[0, pl.ds(r0, c), :].astype(F32)
        o_ref[0, pl.ds(r0, c), :] = (y * (z * _sigmoid(z))).astype(o_ref.dtype)
        return carry

    lax.fori_loop(0, n_chunks, chunk, 0)


def _gla(proj3, code3, w_alpha, b_alpha, gain, *, q_col, k_col, v_col, z_col):
    bsz, seq, _ = proj3.shape
    dk = w_alpha.shape[1] // GLA_HEADS
    dv = gain.shape[1] // GLA_HEADS
    kernel = functools.partial(_gla_kernel, n_chunks=seq // GLA_CHUNK)
    return pl.pallas_call(
        kernel,
        grid=(bsz, GLA_HEADS),
        in_specs=[
            pl.BlockSpec((1, seq, dk), lambda b, h: (b, 0, q_col // dk + h)),
            pl.BlockSpec((1, seq, dk), lambda b, h: (b, 0, k_col // dk + h)),
            pl.BlockSpec((1, seq, dv), lambda b, h: (b, 0, v_col // dv + h)),
            pl.BlockSpec((1, seq, dv), lambda b, h: (b, 0, z_col // dv + h)),
            pl.BlockSpec((1, seq, LANES), lambda b, h: (b, 0, 0)),
            pl.BlockSpec((LANES, dk), lambda b, h: (0, h)),
            pl.BlockSpec((1, dk), lambda b, h: (0, h)),
            pl.BlockSpec((1, dv), lambda b, h: (0, h)),
        ],
        out_specs=pl.BlockSpec((1, seq, dv), lambda b, h: (b, 0, h)),
        out_shape=jax.ShapeDtypeStruct((bsz, seq, GLA_HEADS * dv), BF16),
        scratch_shapes=[pltpu.VMEM((dv, dk), F32), pltpu.VMEM((seq, dk), F32)],
        compiler_params=pltpu.CompilerParams(
            dimension_semantics=("parallel", "parallel"),
            vmem_limit_bytes=VMEM_LIMIT_BYTES),
        name="gla",
    )(proj3, proj3, proj3, proj3, code3, w_alpha, b_alpha, gain)


def _out_kernel(o1_ref, o2_ref, o3_ref, l1_ref, l2_ref, l3_ref, za_ref, ob_ref, ga_ref, gb_ref, x_ref,
                expand_ref, wa_ref, wb_ref, wo_ref, bgate_ref, fgain_ref, out_ref):
    d = x_ref.shape[-1]
    l1, l2, l3 = l1_ref[...], l2_ref[...], l3_ref[...]
    m = jnp.maximum(jnp.maximum(l1, l2), l3)
    e1, e2, e3 = jnp.exp(l1 - m), jnp.exp(l2 - m), jnp.exp(l3 - m)
    inv = 1.0 / (e1 + e2 + e3)
    expand = expand_ref[...]
    oa = (_dot((e1 * inv).astype(BF16), expand) * o1_ref[...].astype(F32)
          + _dot((e2 * inv).astype(BF16), expand) * o2_ref[...].astype(F32)
          + _dot((e3 * inv).astype(BF16), expand) * o3_ref[...].astype(F32))
    za = za_ref[...].astype(F32)
    ya = _dot((oa * (za * _sigmoid(za))).astype(BF16), wa_ref[...])
    yb = _dot(ob_ref[...], wb_ref[...])
    bg = bgate_ref[...]
    gate_a = _sigmoid(ga_ref[...].astype(F32) + bg[:, :d])
    gate_b = _sigmoid(gb_ref[...].astype(F32) + bg[:, d:])
    merged = (gate_a * ya + gate_b * yb).astype(BF16)
    h = x_ref[...] + _dot(merged, wo_ref[...])
    ms = jnp.mean(h * h, axis=-1, keepdims=True)
    out_ref[...] = h * lax.rsqrt(ms + NORM_EPS) * fgain_ref[...]


def _out_stage(o_pats, lse_pats, proj2, ob2, x2, expand, wa, wb, wo, bgate, fgain, *, tm, za_col, ga_col, gb_col):
    t, d = x2.shape
    row_blk = lambda cb: pl.BlockSpec((tm, d), lambda i, cb=cb: (i, cb))
    lse_blk = pl.BlockSpec((tm, LANES), lambda i: (i, 0))
    full = lambda a: pl.BlockSpec(a.shape, lambda i: (0, 0))
    return pl.pallas_call(
        _out_kernel,
        grid=(t // tm,),
        in_specs=[row_blk(0), row_blk(0), row_blk(0), lse_blk, lse_blk, lse_blk,
                  row_blk(za_col // d), row_blk(0), row_blk(ga_col // d), row_blk(gb_col // d), row_blk(0),
                  full(expand), full(wa), full(wb), full(wo), full(bgate), full(fgain)],
        out_specs=row_blk(0),
        out_shape=jax.ShapeDtypeStruct((t, d), F32),
        compiler_params=pltpu.CompilerParams(
            dimension_semantics=("parallel",),
            vmem_limit_bytes=VMEM_LIMIT_BYTES),
        name="out_stage",
    )(*o_pats, *lse_pats, proj2, ob2, proj2, proj2, x2, expand, wa, wb, wo, bgate, fgain)


def kernel(x, norm_gain, w_in, b_gate, w_alpha, b_alpha, gla_norm_gain,
           w_out_attn, w_out_gla, w_out, final_norm_gain):
    bsz, seq, d = x.shape
    assert w_in.shape[0] == 1, "single-layer block"
    aw = ATTN_HEADS * ATTN_HEAD_DIM
    gk = w_alpha.shape[-1]
    gv = gla_norm_gain.shape[-1]
    sizes = (aw, aw, aw, aw, gk, gk, gv, gv, GLA_GATE_RANK, d, d)
    offs = [0]
    for s in sizes:
        offs.append(offs[-1] + s)
    assert offs[-1] == w_in.shape[-1]
    code_lo, code_hi = offs[8], offs[9]

    w = w_in[0]
    w_main = jnp.concatenate([w[:, :code_lo], w[:, code_hi:]], axis=1).astype(BF16)
    w_code = jnp.pad(w[:, code_lo:code_hi], ((0, 0), (0, LANES - GLA_GATE_RANK))).astype(BF16)
    ga_col = code_lo
    gb_col = code_lo + d
    w_alpha_p = jnp.pad(w_alpha[0], ((0, LANES - GLA_GATE_RANK), (0, 0)))
    slopes = 2.0 ** (-8.0 * (jnp.arange(ATTN_HEADS, dtype=F32) + 1.0) / ATTN_HEADS)
    head_of_col = jnp.arange(aw, dtype=jnp.int32) // ATTN_HEAD_DIM
    expand = (jnp.arange(LANES, dtype=jnp.int32)[:, None] == head_of_col[None, :]).astype(BF16)

    x2 = x.reshape(bsz * seq, d)
    proj, code = _in_proj(x2, norm_gain, w_main, w_code, tm=1024, tn=1152)
    proj3 = proj.reshape(bsz, seq, -1)

    pats = [_attention_pattern(proj3, slopes, dilation=dil, pairs=pairs)
            for dil, pairs in zip(DILATIONS, (1, 2, 8))]
    o_pats = [p[0] for p in pats]
    lse_pats = [p[1] for p in pats]

    ob = _gla(proj3, code.reshape(bsz, seq, LANES), w_alpha_p, b_alpha, gla_norm_gain,
              q_col=offs[4], k_col=offs[5], v_col=offs[6], z_col=offs[7])

    out = _out_stage(o_pats, lse_pats, proj, ob.reshape(bsz * seq, gv), x2, expand,
                     w_out_attn[0].astype(BF16), w_out_gla[0].astype(BF16), w_out[0].astype(BF16),
                     b_gate, final_norm_gain.reshape(1, d),
                     tm=256, za_col=offs[3], ga_col=ga_col, gb_col=gb_col)
    return out.reshape(bsz, seq, d)
```

```python
import functools

import jax
import jax.numpy as jnp
from jax import lax
from jax.experimental import pallas as pl
from jax.experimental.pallas import tpu as pltpu

ATTN_HEADS = 16
ATTN_HEAD_DIM = 64
ATTN_WINDOW_STEPS = 128
DILATIONS = (1, 4, 16)
GLA_HEADS = 4
GLA_GATE_RANK = 16
GLA_GATE_TAU = 16.0
GLA_CHUNK = 64
NORM_EPS = 1e-6
GROUP_NORM_EPS = 1e-5

LANES = 128
VMEM_LIMIT_BYTES = 56 * 1024 * 1024

MASK_VALUE = -1e30

F32 = jnp.float32
BF16 = jnp.bfloat16


def _dot(a, b):
    return jnp.dot(a, b, preferred_element_type=F32)


def _dot_nt(a, b):
    return lax.dot_general(a, b, (((1,), (1,)), ((), ())), preferred_element_type=F32)


def _dot_tn(a, b):
    return lax.dot_general(a, b, (((0,), (0,)), ((), ())), preferred_element_type=F32)


def _sigmoid(x):
    return 1.0 / (1.0 + jnp.exp(-x))


def _in_proj_kernel(x_ref, gain_ref, w_ref, wcode_ref, qkv_ref, rest_ref, code_ref, u_ref, *, n_qkv_tiles):
    j = pl.program_id(1)

    @pl.when(j == 0)
    def _():
        x = x_ref[...]
        ms = jnp.mean(x * x, axis=-1, keepdims=True)
        u = (x * lax.rsqrt(ms + NORM_EPS) * gain_ref[...]).astype(BF16)
        u_ref[...] = u
        code_ref[...] = _dot(u, wcode_ref[...])

    acc = _dot(u_ref[...], w_ref[...])

    @pl.when(j < n_qkv_tiles)
    def _():
        qkv_ref[...] = acc

    @pl.when(j >= n_qkv_tiles)
    def _():
        rest_ref[...] = acc.astype(BF16)


def _in_proj(x2, gain, w_main, w_code, *, n_qkv, tm, tn):
    t, d = x2.shape
    n = w_main.shape[1]
    nq = n_qkv // tn
    kernel = functools.partial(_in_proj_kernel, n_qkv_tiles=nq)
    return pl.pallas_call(
        kernel,
        grid=(t // tm, n // tn),
        in_specs=[
            pl.BlockSpec((tm, d), lambda i, j: (i, 0)),
            pl.BlockSpec((1, d), lambda i, j: (0, 0)),
            pl.BlockSpec((d, tn), lambda i, j: (0, j)),
            pl.BlockSpec((d, LANES), lambda i, j: (0, 0)),
        ],
        out_specs=[
            pl.BlockSpec((tm, tn), lambda i, j: (i, jnp.minimum(j, nq - 1))),
            pl.BlockSpec((tm, tn), lambda i, j: (i, jnp.maximum(j - nq, 0))),
            pl.BlockSpec((tm, LANES), lambda i, j: (i, 0)),
        ],
        out_shape=[
            jax.ShapeDtypeStruct((t, n_qkv), F32),
            jax.ShapeDtypeStruct((t, n - n_qkv), BF16),
            jax.ShapeDtypeStruct((t, LANES), F32),
        ],
        scratch_shapes=[pltpu.VMEM((tm, d), BF16)],
        compiler_params=pltpu.CompilerParams(
            dimension_semantics=("parallel", "arbitrary"),
            vmem_limit_bytes=VMEM_LIMIT_BYTES),
        name="in_proj",
    )(x2, gain, w_main, w_code)


def _gather_classes(src_ref, dst_ref, tmp_ref, dilation, scale):
    seq = src_ref.shape[1]
    length = seq // dilation
    if dilation == 1:
        dst_ref[...] = (src_ref[0] * scale).astype(dst_ref.dtype)
    elif dilation == 4:
        for r in range(4):
            rows = src_ref[0, pl.ds(r, length, stride=4), :]
            dst_ref[r * length:(r + 1) * length, :] = (rows * scale).astype(dst_ref.dtype)
    else:
        assert dilation == 16
        quarter = seq // 4
        for r4 in range(4):
            tmp_ref[r4 * quarter:(r4 + 1) * quarter, :] = src_ref[0, pl.ds(r4, quarter, stride=4), :]
        for r4 in range(4):
            for q4 in range(4):
                r16 = r4 + 4 * q4
                rows = tmp_ref[pl.ds(r4 * quarter + q4, length, stride=4), :]
                dst_ref[r16 * length:(r16 + 1) * length, :] = (rows * scale).astype(dst_ref.dtype)


def _attn_kernel(slopes_ref, q_ref, k_ref, v_ref, o_ref, lse_ref, qs_ref, ks_ref, vs_ref, tmp_ref, os_ref,
                 *, dilation, unroll):
    n = ATTN_WINDOW_STEPS
    seq = q_ref.shape[1]
    blocks_per_class = seq // dilation // n
    pair = pl.program_id(1)

    @pl.when(pair == 0)
    def _():
        lse_ref[...] = jnp.zeros_like(lse_ref)

    _gather_classes(q_ref, qs_ref, tmp_ref, dilation, ATTN_HEAD_DIM ** -0.5)
    _gather_classes(k_ref, ks_ref, tmp_ref, dilation, 1.0)
    _gather_classes(v_ref, vs_ref, tmp_ref, dilation, 1.0)

    row = lax.broadcasted_iota(jnp.int32, (2 * n, 2 * n), 0)
    col = lax.broadcasted_iota(jnp.int32, (2 * n, 2 * n), 1)
    steps = n + (row % n) - col
    band = (steps >= 0) & (steps <= n)
    dist = (steps * dilation).astype(F32)
    slope = jnp.where(row < n, slopes_ref[2 * pair], slopes_ref[2 * pair + 1])
    bias = jnp.where(band, -slope * dist, MASK_VALUE)
    bias_first = jnp.where(col < n, MASK_VALUE, bias)
    lane = lax.broadcasted_iota(jnp.int32, (n, LANES), 1)
    low_half = lane < ATTN_HEAD_DIM

    def scores(g):
        j = g % blocks_per_class
        r0 = pl.multiple_of(g * n, n)
        rp = pl.multiple_of(jnp.maximum(g - 1, 0) * n, n)
        q = qs_ref[pl.ds(r0, n), :]
        zero = jnp.zeros_like(q)
        qq = jnp.concatenate([jnp.where(low_half, q, zero), jnp.where(low_half, zero, q)], axis=0)
        kk = jnp.concatenate([ks_ref[pl.ds(rp, n), :], ks_ref[pl.ds(r0, n), :]], axis=0)
        return _dot_nt(qq, kk) + jnp.where(j == 0, bias_first, bias)

    def softmax(s):
        m = jnp.max(s, axis=-1, keepdims=True)
        e = jnp.exp(s - m)
        den = jnp.sum(e, axis=-1, keepdims=True)
        return e.astype(BF16), m, den

    def finish(g, e, m, den):
        j = g % blocks_per_class
        r0 = pl.multiple_of(g * n, n)
        rp = pl.multiple_of(jnp.maximum(g - 1, 0) * n, n)
        vv = jnp.concatenate([vs_ref[pl.ds(rp, n), :], vs_ref[pl.ds(r0, n), :]], axis=0)
        pv = _dot(e, vv) / den
        lse = m + jnp.log(den)
        start = j * (n * dilation) + g // blocks_per_class
        rows = pl.ds(start, n, stride=dilation) if dilation > 1 else pl.ds(r0, n)
        os_ref[rows, :] = jnp.where(low_half, pv[:n], pv[n:])
        cur = lse_ref[0, rows, :]
        cur = jnp.where(lane == 2 * pair, lse[:n], cur)
        cur = jnp.where(lane == 2 * pair + 1, lse[n:], cur)
        lse_ref[0, rows, :] = cur

    def qblocks(it, carry):
        g0 = it * unroll
        s_next = scores(g0)
        for u in range(unroll):
            s_cur = s_next
            if u + 1 < unroll:
                s_next = scores(g0 + u + 1)
            finish(g0 + u, *softmax(s_cur))
        return carry

    lax.fori_loop(0, seq // n // unroll, qblocks, 0)
    o_ref[0] = os_ref[...].astype(o_ref.dtype)


def _attention_pattern(qkv3, slopes, *, dilation, unroll):
    bsz, seq, ncols = qkv3.shape
    width = ncols // 3
    assert seq % (ATTN_WINDOW_STEPS * dilation) == 0
    kb = width // LANES

    def in_map(section):
        return lambda b, s, *_: (b, 0, section * kb + s)

    kernel = functools.partial(_attn_kernel, dilation=dilation, unroll=unroll)
    o, lse = pl.pallas_call(
        kernel,
        grid_spec=pltpu.PrefetchScalarGridSpec(
            num_scalar_prefetch=1,
            grid=(bsz, kb),
            in_specs=[
                pl.BlockSpec((1, seq, LANES), in_map(0)),
                pl.BlockSpec((1, seq, LANES), in_map(1)),
                pl.BlockSpec((1, seq, LANES), in_map(2)),
            ],
            out_specs=[
                pl.BlockSpec((1, seq, LANES), lambda b, s, *_: (b, 0, s)),
                pl.BlockSpec((1, seq, LANES), lambda b, s, *_: (b, 0, 0)),
            ],
            scratch_shapes=[
                pltpu.VMEM((seq, LANES), BF16),
                pltpu.VMEM((seq, LANES), BF16),
                pltpu.VMEM((seq, LANES), BF16),
                pltpu.VMEM((seq, LANES), F32),
                pltpu.VMEM((seq, LANES), F32),
            ],
        ),
        out_shape=[
            jax.ShapeDtypeStruct((bsz, seq, width), BF16),
            jax.ShapeDtypeStruct((bsz, seq, LANES), F32),
        ],
        compiler_params=pltpu.CompilerParams(
            dimension_semantics=("parallel", "arbitrary"),
            vmem_limit_bytes=VMEM_LIMIT_BYTES),
        name=f"dilated_attn_d{dilation}",
    )(slopes, qkv3, qkv3, qkv3)
    return o.reshape(bsz * seq, width), lse.reshape(bsz * seq, LANES)


def _split_bf16(x):
    hi = x.astype(BF16)
    return hi, (x - hi.astype(F32)).astype(BF16)


def _gla_kernel(q_ref, k_ref, v_ref, z_ref, code_ref, walpha_ref, balpha_ref, gain_ref,
                o_ref, state_ref, loga_ref, *, n_chunks, unroll):
    c = GLA_CHUNK
    dk = q_ref.shape[-1]

    code_hi, code_lo = _split_bf16(code_ref[0])
    w_hi, w_lo = _split_bf16(walpha_ref[...])
    logits = _dot(code_hi, w_hi) + _dot(code_hi, w_lo) + _dot(code_lo, w_hi) + balpha_ref[...]
    log_sig = jnp.minimum(logits, 0.0) - jnp.log1p(jnp.exp(-jnp.abs(logits)))
    loga_ref[...] = log_sig / GLA_GATE_TAU

    state_ref[...] = jnp.zeros_like(state_ref)
    ri = lax.broadcasted_iota(jnp.int32, (c, c), 0)
    ci = lax.broadcasted_iota(jnp.int32, (c, c), 1)
    causal = ri >= ci
    tri = causal.astype(BF16)
    gain = gain_ref[...]

    def chunk(ic, carry):
        r0 = pl.multiple_of(ic * c, c)
        la = loga_ref[pl.ds(r0, c), :]
        la1 = la.astype(BF16)
        rem = la - la1.astype(F32)
        la2 = rem.astype(BF16)
        la3 = (rem - la2.astype(F32)).astype(BF16)
        b = _dot(tri, la1) + _dot(tri, la2) + _dot(tri, la3)
        b_last = b[c - 1:c, :]
        q = q_ref[0, pl.ds(r0, c), :].astype(F32) * (dk ** -0.5)
        k = k_ref[0, pl.ds(r0, c), :].astype(F32)
        v = v_ref[0, pl.ds(r0, c), :]
        q_dec = (q * jnp.exp(b)).astype(BF16)
        k_inv = (k * jnp.exp(-b)).astype(BF16)
        k_end = (k * jnp.exp(b_last - b)).astype(BF16)
        a = jnp.where(causal, _dot_nt(q_dec, k_inv), 0.0)
        st = state_ref[...]
        o = _dot(a.astype(BF16), v) + _dot_nt(q_dec, st.astype(BF16))
        state_ref[...] = jnp.exp(b_last) * st + _dot_tn(v, k_end)
        mu = jnp.mean(o, axis=-1, keepdims=True)
        cen = o - mu
        var = jnp.mean(cen * cen, axis=-1, keepdims=True)
        y = cen * lax.rsqrt(var + GROUP_NORM_EPS) * gain
        z = z_ref[0, pl.ds(r0, c), :].astype(F32)
        o_ref[0, pl.ds(r0, c), :] = (y * (z * _sigmoid(z))).astype(o_ref.dtype)
        return carry

    lax.fori_loop(0, n_chunks, chunk, 0, unroll=unroll)


def _gla(proj3, code3, w_alpha, b_alpha, gain, *, q_col, k_col, v_col, z_col, unroll):
    bsz, seq, _ = proj3.shape
    dk = w_alpha.shape[1] // GLA_HEADS
    dv = gain.shape[1] // GLA_HEADS
    kernel = functools.partial(_gla_kernel, n_chunks=seq // GLA_CHUNK, unroll=unroll)
    return pl.pallas_call(
        kernel,
        grid=(bsz, GLA_HEADS),
        in_specs=[
            pl.BlockSpec((1, seq, dk), lambda b, h: (b, 0, q_col // dk + h)),
            pl.BlockSpec((1, seq, dk), lambda b, h: (b, 0, k_col // dk + h)),
            pl.BlockSpec((1, seq, dv), lambda b, h: (b, 0, v_col // dv + h)),
            pl.BlockSpec((1, seq, dv), lambda b, h: (b, 0, z_col // dv + h)),
            pl.BlockSpec((1, seq, LANES), lambda b, h: (b, 0, 0)),
            pl.BlockSpec((LANES, dk), lambda b, h: (0, h)),
            pl.BlockSpec((1, dk), lambda b, h: (0, h)),
            pl.BlockSpec((1, dv), lambda b, h: (0, h)),
        ],
        out_specs=pl.BlockSpec((1, seq, dv), lambda b, h: (b, 0, h)),
        out_shape=jax.ShapeDtypeStruct((bsz, seq, GLA_HEADS * dv), BF16),
        scratch_shapes=[pltpu.VMEM((dv, dk), F32), pltpu.VMEM((seq, dk), F32)],
        compiler_params=pltpu.CompilerParams(
            dimension_semantics=("parallel", "parallel"),
            vmem_limit_bytes=VMEM_LIMIT_BYTES),
        name="gla",
    )(proj3, proj3, proj3, proj3, code3, w_alpha, b_alpha, gain)


def _out_kernel(o1_ref, o2_ref, o3_ref, l1_ref, l2_ref, l3_ref, za_ref, ob_ref, ga_ref, gb_ref, x_ref,
                expand_ref, wa_ref, wb_ref, wo_ref, bgate_ref, fgain_ref, out_ref):
    d = x_ref.shape[-1]
    l1, l2, l3 = l1_ref[...], l2_ref[...], l3_ref[...]
    m = jnp.maximum(jnp.maximum(l1, l2), l3)
    e1, e2, e3 = jnp.exp(l1 - m), jnp.exp(l2 - m), jnp.exp(l3 - m)
    inv = 1.0 / (e1 + e2 + e3)
    expand = expand_ref[...]
    oa = (_dot((e1 * inv).astype(BF16), expand) * o1_ref[...].astype(F32)
          + _dot((e2 * inv).astype(BF16), expand) * o2_ref[...].astype(F32)
          + _dot((e3 * inv).astype(BF16), expand) * o3_ref[...].astype(F32))
    za = za_ref[...].astype(F32)
    ya = _dot((oa * (za * _sigmoid(za))).astype(BF16), wa_ref[...])
    yb = _dot(ob_ref[...], wb_ref[...])
    bg = bgate_ref[...]
    gate_a = _sigmoid(ga_ref[...].astype(F32) + bg[:, :d])
    gate_b = _sigmoid(gb_ref[...].astype(F32) + bg[:, d:])
    merged = (gate_a * ya + gate_b * yb).astype(BF16)
    h = x_ref[...] + _dot(merged, wo_ref[...])
    ms = jnp.mean(h * h, axis=-1, keepdims=True)
    out_ref[...] = h * lax.rsqrt(ms + NORM_EPS) * fgain_ref[...]


def _out_stage(o_pats, lse_pats, proj2, ob2, x2, expand, wa, wb, wo, bgate, fgain, *, tm, za_col, ga_col, gb_col):
    t, d = x2.shape
    row_blk = lambda cb: pl.BlockSpec((tm, d), lambda i, cb=cb: (i, cb))
    lse_blk = pl.BlockSpec((tm, LANES), lambda i: (i, 0))
    full = lambda a: pl.BlockSpec(a.shape, lambda i: (0, 0))
    return pl.pallas_call(
        _out_kernel,
        grid=(t // tm,),
        in_specs=[row_blk(0), row_blk(0), row_blk(0), lse_blk, lse_blk, lse_blk,
                  row_blk(za_col // d), row_blk(0), row_blk(ga_col // d), row_blk(gb_col // d), row_blk(0),
                  full(expand), full(wa), full(wb), full(wo), full(bgate), full(fgain)],
        out_specs=row_blk(0),
        out_shape=jax.ShapeDtypeStruct((t, d), F32),
        compiler_params=pltpu.CompilerParams(
            dimension_semantics=("parallel",),
            vmem_limit_bytes=VMEM_LIMIT_BYTES),
        name="out_stage",
    )(*o_pats, *lse_pats, proj2, ob2, proj2, proj2, x2, expand, wa, wb, wo, bgate, fgain)


def kernel(x, norm_gain, w_in, b_gate, w_alpha, b_alpha, gla_norm_gain,
           w_out_attn, w_out_gla, w_out, final_norm_gain):
    bsz, seq, d = x.shape
    assert w_in.shape[0] == 1, "single-layer block"
    aw = ATTN_HEADS * ATTN_HEAD_DIM
    gk = w_alpha.shape[-1]
    gv = gla_norm_gain.shape[-1]
    sizes = (aw, aw, aw, aw, gk, gk, gv, gv, GLA_GATE_RANK, d, d)
    offs = [0]
    for s in sizes:
        offs.append(offs[-1] + s)
    assert offs[-1] == w_in.shape[-1]
    code_lo, code_hi = offs[8], offs[9]
    n_qkv = offs[3]

    w = w_in[0]
    w_main = jnp.concatenate([w[:, :code_lo], w[:, code_hi:]], axis=1).astype(BF16)
    w_code = jnp.pad(w[:, code_lo:code_hi], ((0, 0), (0, LANES - GLA_GATE_RANK))).astype(BF16)
    w_alpha_p = jnp.pad(w_alpha[0], ((0, LANES - GLA_GATE_RANK), (0, 0)))
    slopes = 2.0 ** (-8.0 * (jnp.arange(ATTN_HEADS, dtype=F32) + 1.0) / ATTN_HEADS)
    head_of_col = jnp.arange(aw, dtype=jnp.int32) // ATTN_HEAD_DIM
    expand = (jnp.arange(LANES, dtype=jnp.int32)[:, None] == head_of_col[None, :]).astype(BF16)

    x2 = x.reshape(bsz * seq, d)
    qkv, rest, code = _in_proj(x2, norm_gain, w_main, w_code, n_qkv=n_qkv, tm=1024, tn=1024)
    qkv3 = qkv.reshape(bsz, seq, n_qkv)
    rest3 = rest.reshape(bsz, seq, -1)
    rel = lambda o: o - n_qkv - (GLA_GATE_RANK if o > code_lo else 0)

    pats = [_attention_pattern(qkv3, slopes, dilation=dil, unroll=4) for dil in DILATIONS]
    o_pats = [p[0] for p in pats]
    lse_pats = [p[1] for p in pats]

    ob = _gla(rest3, code.reshape(bsz, seq, LANES), w_alpha_p, b_alpha, gla_norm_gain,
              q_col=rel(offs[4]), k_col=rel(offs[5]), v_col=rel(offs[6]), z_col=rel(offs[7]), unroll=4)

    out = _out_stage(o_pats, lse_pats, rest, ob.reshape(bsz * seq, gv), x2, expand,
                     w_out_attn[0].astype(BF16), w_out_gla[0].astype(BF16), w_out[0].astype(BF16),
                     b_gate, final_norm_gain.reshape(1, d),
                     tm=256, za_col=rel(offs[3]), ga_col=rel(offs[9]), gb_col=rel(offs[10]))
    return out.reshape(bsz, seq, d)
```

```python
import functools

import jax
import jax.numpy as jnp
from jax import lax
from jax.experimental import pallas as pl
from jax.experimental.pallas import tpu as pltpu

ATTN_HEADS = 16
ATTN_HEAD_DIM = 64
ATTN_WINDOW_STEPS = 128
DILATIONS = (1, 4, 16)
GLA_HEADS = 4
GLA_GATE_RANK = 16
GLA_GATE_TAU = 16.0
GLA_CHUNK = 64
NORM_EPS = 1e-6
GROUP_NORM_EPS = 1e-5

LANES = 128
VMEM_LIMIT_BYTES = 56 * 1024 * 1024

MASK_VALUE = -1e30
DEN_LANE = ATTN_HEADS

F32 = jnp.float32
BF16 = jnp.bfloat16


def _dot(a, b):
    return jnp.dot(a, b, preferred_element_type=F32)


def _dot_nt(a, b):
    return lax.dot_general(a, b, (((1,), (1,)), ((), ())), preferred_element_type=F32)


def _dot_tn(a, b):
    return lax.dot_general(a, b, (((0,), (0,)), ((), ())), preferred_element_type=F32)


def _sigmoid(x):
    return 1.0 / (1.0 + jnp.exp(-x))


def _in_proj_kernel(*refs, with_code):
    if with_code:
        x_ref, gain_ref, w_ref, wcode_ref, out_ref, code_ref, u_ref = refs
    else:
        x_ref, gain_ref, w_ref, out_ref, u_ref = refs

    @pl.when(pl.program_id(1) == 0)
    def _():
        x = x_ref[...]
        ms = jnp.mean(x * x, axis=-1, keepdims=True)
        u = (x * lax.rsqrt(ms + NORM_EPS) * gain_ref[...]).astype(BF16)
        u_ref[...] = u
        if with_code:
            code_ref[...] = _dot(u, wcode_ref[...])

    out_ref[...] = _dot(u_ref[...], w_ref[...]).astype(out_ref.dtype)


def _in_proj(x2, gain, w, w_code, *, out_dtype, tm, tn):
    t, d = x2.shape
    n = w.shape[1]
    with_code = w_code is not None
    in_specs = [
        pl.BlockSpec((tm, d), lambda i, j: (i, 0)),
        pl.BlockSpec((1, d), lambda i, j: (0, 0)),
        pl.BlockSpec((d, tn), lambda i, j: (0, j)),
    ]
    out_specs = [pl.BlockSpec((tm, tn), lambda i, j: (i, j))]
    out_shape = [jax.ShapeDtypeStruct((t, n), out_dtype)]
    operands = [x2, gain, w]
    if with_code:
        in_specs.append(pl.BlockSpec((d, LANES), lambda i, j: (0, 0)))
        out_specs.append(pl.BlockSpec((tm, LANES), lambda i, j: (i, 0)))
        out_shape.append(jax.ShapeDtypeStruct((t, LANES), F32))
        operands.append(w_code)
    return pl.pallas_call(
        functools.partial(_in_proj_kernel, with_code=with_code),
        grid=(t // tm, n // tn),
        in_specs=in_specs,
        out_specs=out_specs,
        out_shape=out_shape,
        scratch_shapes=[pltpu.VMEM((tm, d), BF16)],
        compiler_params=pltpu.CompilerParams(
            dimension_semantics=("parallel", "arbitrary"),
            vmem_limit_bytes=VMEM_LIMIT_BYTES),
        name="in_proj_qkv" if with_code else "in_proj_rest",
    )(*operands)


def _gather_classes(src_ref, dst_ref, tmp_ref, dilation, scale):
    seq = src_ref.shape[1]
    length = seq // dilation
    if dilation == 1:
        dst_ref[...] = (src_ref[0] * scale).astype(dst_ref.dtype)
    elif dilation == 4:
        for r in range(4):
            rows = src_ref[0, pl.ds(r, length, stride=4), :]
            dst_ref[r * length:(r + 1) * length, :] = (rows * scale).astype(dst_ref.dtype)
    else:
        assert dilation == 16
        quarter = seq // 4
        for r4 in range(4):
            tmp_ref[r4 * quarter:(r4 + 1) * quarter, :] = src_ref[0, pl.ds(r4, quarter, stride=4), :]
        for r4 in range(4):
            for q4 in range(4):
                r16 = r4 + 4 * q4
                rows = tmp_ref[pl.ds(r4 * quarter + q4, length, stride=4), :]
                dst_ref[r16 * length:(r16 + 1) * length, :] = (rows * scale).astype(dst_ref.dtype)


def _scatter_classes(src_ref, tmp_ref, dilation):
    seq = src_ref.shape[0]
    length = seq // dilation
    if dilation == 1:
        return src_ref
    if dilation == 4:
        for r in range(4):
            tmp_ref[pl.ds(r, length, stride=4), :] = src_ref[r * length:(r + 1) * length, :]
        return tmp_ref
    assert dilation == 16
    quarter = seq // 4
    for r4 in range(4):
        for q4 in range(4):
            r16 = r4 + 4 * q4
            tmp_ref[pl.ds(r4 * quarter + q4, length, stride=4), :] = src_ref[r16 * length:(r16 + 1) * length, :]
    for r4 in range(4):
        src_ref[pl.ds(r4, quarter, stride=4), :] = tmp_ref[r4 * quarter:(r4 + 1) * quarter, :]
    return src_ref


def _attn_kernel(slopes_ref, q_ref, k_ref, v_ref, o_ref, stats_ref,
                 qs_ref, ks_ref, vs_ref, tmp_ref, acc_ref, st_ref, *, dilation, unroll):
    n = ATTN_WINDOW_STEPS
    seq = q_ref.shape[1]
    blocks_per_class = seq // dilation // n
    pair = pl.program_id(1)

    @pl.when(pair == 0)
    def _():
        st_ref[...] = jnp.zeros_like(st_ref)

    _gather_classes(q_ref, qs_ref, tmp_ref, dilation, ATTN_HEAD_DIM ** -0.5)
    _gather_classes(k_ref, ks_ref, tmp_ref, dilation, 1.0)
    _gather_classes(v_ref, vs_ref, tmp_ref, dilation, 1.0)

    row = lax.broadcasted_iota(jnp.int32, (2 * n, 2 * n), 0)
    col = lax.broadcasted_iota(jnp.int32, (2 * n, 2 * n), 1)
    steps = n + (row % n) - col
    band = (steps >= 0) & (steps <= n)
    dist = (steps * dilation).astype(F32)
    slope = jnp.where(row < n, slopes_ref[2 * pair], slopes_ref[2 * pair + 1])
    bias = jnp.where(band, -slope * dist, MASK_VALUE)
    bias_first = jnp.where(col < n, MASK_VALUE, bias)
    lane = lax.broadcasted_iota(jnp.int32, (n, LANES), 1)
    low_half = lane < ATTN_HEAD_DIM
    is_m0, is_m1 = lane == 2 * pair, lane == 2 * pair + 1
    is_d0, is_d1 = lane == DEN_LANE + 2 * pair, lane == DEN_LANE + 2 * pair + 1

    def scores(g):
        r0 = pl.multiple_of(g * n, n)
        rp = pl.multiple_of(jnp.maximum(g - 1, 0) * n, n)
        q = qs_ref[pl.ds(r0, n), :]
        zero = jnp.zeros_like(q)
        qq = jnp.concatenate([jnp.where(low_half, q, zero), jnp.where(low_half, zero, q)], axis=0)
        kk = jnp.concatenate([ks_ref[pl.ds(rp, n), :], ks_ref[pl.ds(r0, n), :]], axis=0)
        first = g % blocks_per_class == 0
        return _dot_nt(qq, kk) + jnp.where(first, bias_first, bias)

    def softmax(s):
        m = jnp.max(s, axis=-1, keepdims=True)
        e = jnp.exp(s - m)
        den = jnp.sum(e, axis=-1, keepdims=True)
        return e.astype(BF16), m, den

    def finish(g, e, m, den):
        r0 = pl.multiple_of(g * n, n)
        rp = pl.multiple_of(jnp.maximum(g - 1, 0) * n, n)
        vv = jnp.concatenate([vs_ref[pl.ds(rp, n), :], vs_ref[pl.ds(r0, n), :]], axis=0)
        pv = _dot(e, vv)
        acc_ref[pl.ds(r0, n), :] = jnp.where(low_half, pv[:n], pv[n:])
        cur = st_ref[pl.ds(r0, n), :]
        cur = jnp.where(is_m0, m[:n], jnp.where(is_m1, m[n:], cur))
        cur = jnp.where(is_d0, den[:n], jnp.where(is_d1, den[n:], cur))
        st_ref[pl.ds(r0, n), :] = cur

    def qblocks(it, carry):
        g0 = it * unroll
        s_next = scores(g0)
        for u in range(unroll):
            s_cur = s_next
            if u + 1 < unroll:
                s_next = scores(g0 + u + 1)
            finish(g0 + u, *softmax(s_cur))
        return carry

    lax.fori_loop(0, seq // n // unroll, qblocks, 0)

    o_ref[0] = _scatter_classes(acc_ref, tmp_ref, dilation)[...].astype(o_ref.dtype)

    @pl.when(pair == pl.num_programs(1) - 1)
    def _():
        stats_ref[0] = _scatter_classes(st_ref, tmp_ref, dilation)[...]


def _attention_pattern(qkv3, slopes, *, dilation, unroll):
    bsz, seq, ncols = qkv3.shape
    width = ncols // 3
    assert seq % (ATTN_WINDOW_STEPS * dilation) == 0
    kb = width // LANES

    def in_map(section):
        return lambda b, s, *_: (b, 0, section * kb + s)

    kernel = functools.partial(_attn_kernel, dilation=dilation, unroll=unroll)
    o, stats = pl.pallas_call(
        kernel,
        grid_spec=pltpu.PrefetchScalarGridSpec(
            num_scalar_prefetch=1,
            grid=(bsz, kb),
            in_specs=[
                pl.BlockSpec((1, seq, LANES), in_map(0)),
                pl.BlockSpec((1, seq, LANES), in_map(1)),
                pl.BlockSpec((1, seq, LANES), in_map(2)),
            ],
            out_specs=[
                pl.BlockSpec((1, seq, LANES), lambda b, s, *_: (b, 0, s)),
                pl.BlockSpec((1, seq, LANES), lambda b, s, *_: (b, 0, 0)),
            ],
            scratch_shapes=[
                pltpu.VMEM((seq, LANES), BF16),
                pltpu.VMEM((seq, LANES), BF16),
                pltpu.VMEM((seq, LANES), BF16),
                pltpu.VMEM((seq, LANES), F32),
                pltpu.VMEM((seq, LANES), F32),
                pltpu.VMEM((seq, LANES), F32),
            ],
        ),
        out_shape=[
            jax.ShapeDtypeStruct((bsz, seq, width), BF16),
            jax.ShapeDtypeStruct((bsz, seq, LANES), F32),
        ],
        compiler_params=pltpu.CompilerParams(
            dimension_semantics=("parallel", "arbitrary"),
            vmem_limit_bytes=VMEM_LIMIT_BYTES),
        name=f"dilated_attn_d{dilation}",
    )(slopes, qkv3, qkv3, qkv3)
    return o.reshape(bsz * seq, width), stats.reshape(bsz * seq, LANES)


def _split_bf16(x):
    hi = x.astype(BF16)
    return hi, (x - hi.astype(F32)).astype(BF16)


def _gla_kernel(q_ref, k_ref, v_ref, z_ref, code_ref, walpha_ref, balpha_ref, gain_ref,
                o_ref, state_ref, b_ref, *, n_chunks, group):
    c = GLA_CHUNK
    dk = q_ref.shape[-1]

    code_hi, code_lo = _split_bf16(code_ref[0])
    w_hi, w_lo = _split_bf16(walpha_ref[...])
    logits = _dot(code_hi, w_hi) + _dot(code_hi, w_lo) + _dot(code_lo, w_hi) + balpha_ref[...]
    log_sig = jnp.minimum(logits, 0.0) - jnp.log(1.0 + jnp.exp(-jnp.abs(logits)))
    b_ref[...] = log_sig / GLA_GATE_TAU

    ri = lax.broadcasted_iota(jnp.int32, (c, c), 0)
    ci = lax.broadcasted_iota(jnp.int32, (c, c), 1)
    causal = ri >= ci
    tri = causal.astype(BF16)
    gain = gain_ref[...]

    def cumsum_chunk(ic, carry):
        rows = pl.ds(pl.multiple_of(ic * c, c), c)
        la = b_ref[rows, :]
        la1 = la.astype(BF16)
        rem = la - la1.astype(F32)
        la2 = rem.astype(BF16)
        la3 = (rem - la2.astype(F32)).astype(BF16)
        b_ref[rows, :] = _dot(tri, la1) + _dot(tri, la2) + _dot(tri, la3)
        return carry

    lax.fori_loop(0, n_chunks, cumsum_chunk, 0, unroll=8)

    state_ref[...] = jnp.zeros_like(state_ref)

    def chunk_group(ig, carry):
        rows = [pl.ds(pl.multiple_of((ig * group + u) * c, c), c) for u in range(group)]
        q_dec, k_inv, k_end, decay, v = [], [], [], [], []
        for u in range(group):
            b = b_ref[rows[u], :]
            b_last = b[c - 1:c, :]
            q = q_ref[0, rows[u], :].astype(F32) * (dk ** -0.5)
            k = k_ref[0, rows[u], :].astype(F32)
            q_dec.append((q * jnp.exp(b)).astype(BF16))
            k_inv.append((k * jnp.exp(-b)).astype(BF16))
            k_end.append((k * jnp.exp(b_last - b)).astype(BF16))
            decay.append(jnp.exp(b_last))
            v.append(v_ref[0, rows[u], :])
        a = [jnp.where(causal, _dot_nt(q_dec[u], k_inv[u]), 0.0).astype(BF16) for u in range(group)]
        d_state = [_dot_tn(v[u], k_end[u]) for u in range(group)]
        o_intra = [_dot(a[u], v[u]) for u in range(group)]
        st = state_ref[...]
        o = []
        for u in range(group):
            o.append(o_intra[u] + _dot_nt(q_dec[u], st.astype(BF16)))
            st = decay[u] * st + d_state[u]
        state_ref[...] = st
        for u in range(group):
            mu = jnp.mean(o[u], axis=-1, keepdims=True)
            cen = o[u] - mu
            var = jnp.mean(cen * cen, axis=-1, keepdims=True)
            y = cen * lax.rsqrt(var + GROUP_NORM_EPS) * gain
            z = z_ref[0, rows[u], :].astype(F32)
            o_ref[0, rows[u], :] = (y * (z * _sigmoid(z))).astype(o_ref.dtype)
        return carry

    lax.fori_loop(0, n_chunks // group, chunk_group, 0)


def _gla(proj3, code3, w_alpha, b_alpha, gain, *, q_col, k_col, v_col, z_col, group):
    bsz, seq, _ = proj3.shape
    dk = w_alpha.shape[1] // GLA_HEADS
    dv = gain.shape[1] // GLA_HEADS
    kernel = functools.partial(_gla_kernel, n_chunks=seq // GLA_CHUNK, group=group)
    return pl.pallas_call(
        kernel,
        grid=(bsz, GLA_HEADS),
        in_specs=[
            pl.BlockSpec((1, seq, dk), lambda b, h: (b, 0, q_col // dk + h)),
            pl.BlockSpec((1, seq, dk), lambda b, h: (b, 0, k_col // dk + h)),
            pl.BlockSpec((1, seq, dv), lambda b, h: (b, 0, v_col // dv + h)),
            pl.BlockSpec((1, seq, dv), lambda b, h: (b, 0, z_col // dv + h)),
            pl.BlockSpec((1, seq, LANES), lambda b, h: (b, 0, 0)),
            pl.BlockSpec((LANES, dk), lambda b, h: (0, h)),
            pl.BlockSpec((1, dk), lambda b, h: (0, h)),
            pl.BlockSpec((1, dv), lambda b, h: (0, h)),
        ],
        out_specs=pl.BlockSpec((1, seq, dv), lambda b, h: (b, 0, h)),
        out_shape=jax.ShapeDtypeStruct((bsz, seq, GLA_HEADS * dv), BF16),
        scratch_shapes=[pltpu.VMEM((dv, dk), F32), pltpu.VMEM((seq, dk), F32)],
        compiler_params=pltpu.CompilerParams(
            dimension_semantics=("parallel", "parallel"),
            vmem_limit_bytes=VMEM_LIMIT_BYTES),
        name="gla",
    )(proj3, proj3, proj3, proj3, code3, w_alpha, b_alpha, gain)


def _out_kernel(a1_ref, a2_ref, a3_ref, s1_ref, s2_ref, s3_ref, za_ref, ob_ref, ga_ref, gb_ref, x_ref,
                expand_ref, wa_ref, wb_ref, wo_ref, bgate_ref, fgain_ref, out_ref, *, chunks):
    tm, d = x_ref.shape
    rc = tm // chunks
    expand = expand_ref[...]
    bg = bgate_ref[...]
    head_lane = lax.broadcasted_iota(jnp.int32, (rc, LANES), 1) < ATTN_HEADS
    ya, yb = [], []
    for c in range(chunks):
        rows = slice(c * rc, (c + 1) * rc)
        stats = [s_ref[rows, :] for s_ref in (s1_ref, s2_ref, s3_ref)]
        m = jnp.maximum(jnp.maximum(stats[0], stats[1]), stats[2])
        scale = [jnp.exp(s - m) for s in stats]
        den = [pltpu.roll(s, LANES - DEN_LANE, 1) for s in stats]
        inv = 1.0 / (scale[0] * den[0] + scale[1] * den[1] + scale[2] * den[2])
        oa = None
        for p, a_ref in enumerate((a1_ref, a2_ref, a3_ref)):
            weight = jnp.where(head_lane, scale[p] * inv, 0.0).astype(BF16)
            term = _dot(weight, expand) * a_ref[rows, :].astype(F32)
            oa = term if oa is None else oa + term
        za = za_ref[rows, :].astype(F32)
        ya.append(_dot((oa * (za * _sigmoid(za))).astype(BF16), wa_ref[...]))
        yb.append(_dot(ob_ref[rows, :], wb_ref[...]))
    h = []
    for c in range(chunks):
        rows = slice(c * rc, (c + 1) * rc)
        gate_a = _sigmoid(ga_ref[rows, :].astype(F32) + bg[:, :d])
        gate_b = _sigmoid(gb_ref[rows, :].astype(F32) + bg[:, d:])
        merged = (gate_a * ya[c] + gate_b * yb[c]).astype(BF16)
        h.append(x_ref[rows, :] + _dot(merged, wo_ref[...]))
    for c in range(chunks):
        rows = slice(c * rc, (c + 1) * rc)
        ms = jnp.mean(h[c] * h[c], axis=-1, keepdims=True)
        out_ref[rows, :] = h[c] * lax.rsqrt(ms + NORM_EPS) * fgain_ref[...]


def _out_stage(a_pats, s_pats, proj2, ob2, x2, expand, wa, wb, wo, bgate, fgain, *,
               tm, chunks, za_col, ga_col, gb_col):
    t, d = x2.shape
    row_blk = lambda cb: pl.BlockSpec((tm, d), lambda i, cb=cb: (i, cb))
    stat_blk = pl.BlockSpec((tm, LANES), lambda i: (i, 0))
    full = lambda a: pl.BlockSpec(a.shape, lambda i: (0, 0))
    return pl.pallas_call(
        functools.partial(_out_kernel, chunks=chunks),
        grid=(t // tm,),
        in_specs=[row_blk(0), row_blk(0), row_blk(0), stat_blk, stat_blk, stat_blk,
                  row_blk(za_col // d), row_blk(0), row_blk(ga_col // d), row_blk(gb_col // d), row_blk(0),
                  full(expand), full(wa), full(wb), full(wo), full(bgate), full(fgain)],
        out_specs=row_blk(0),
        out_shape=jax.ShapeDtypeStruct((t, d), F32),
        compiler_params=pltpu.CompilerParams(
            dimension_semantics=("parallel",),
            vmem_limit_bytes=VMEM_LIMIT_BYTES),
        name="out_stage",
    )(*a_pats, *s_pats, proj2, ob2, proj2, proj2, x2, expand, wa, wb, wo, bgate, fgain)


def kernel(x, norm_gain, w_in, b_gate, w_alpha, b_alpha, gla_norm_gain,
           w_out_attn, w_out_gla, w_out, final_norm_gain):
    bsz, seq, d = x.shape
    assert w_in.shape[0] == 1, "single-layer block"
    aw = ATTN_HEADS * ATTN_HEAD_DIM
    gk = w_alpha.shape[-1]
    gv = gla_norm_gain.shape[-1]
    sizes = (aw, aw, aw, aw, gk, gk, gv, gv, GLA_GATE_RANK, d, d)
    offs = [0]
    for s in sizes:
        offs.append(offs[-1] + s)
    assert offs[-1] == w_in.shape[-1]
    code_lo, code_hi = offs[8], offs[9]
    n_qkv = offs[3]

    w = w_in[0]
    w_main = jnp.concatenate([w[:, :code_lo], w[:, code_hi:]], axis=1).astype(BF16)
    w_code = jnp.pad(w[:, code_lo:code_hi], ((0, 0), (0, LANES - GLA_GATE_RANK))).astype(BF16)
    w_alpha_p = jnp.pad(w_alpha[0], ((0, LANES - GLA_GATE_RANK), (0, 0)))
    slopes = 2.0 ** (-8.0 * (jnp.arange(ATTN_HEADS, dtype=F32) + 1.0) / ATTN_HEADS)
    head_of_col = jnp.arange(aw, dtype=jnp.int32) // ATTN_HEAD_DIM
    expand = (jnp.arange(LANES, dtype=jnp.int32)[:, None] == head_of_col[None, :]).astype(BF16)

    x2 = x.reshape(bsz * seq, d)
    qkv, code = _in_proj(x2, norm_gain, w_main[:, :n_qkv], w_code, out_dtype=F32, tm=1024, tn=1536)
    rest, = _in_proj(x2, norm_gain, w_main[:, n_qkv:], None, out_dtype=BF16, tm=1024, tn=1536)
    qkv3 = qkv.reshape(bsz, seq, n_qkv)
    rest3 = rest.reshape(bsz, seq, -1)
    rel = lambda o: o - n_qkv - (GLA_GATE_RANK if o > code_lo else 0)

    pats = [_attention_pattern(qkv3, slopes, dilation=dil, unroll=8) for dil in DILATIONS]
    a_pats = [p[0] for p in pats]
    s_pats = [p[1] for p in pats]

    ob = _gla(rest3, code.reshape(bsz, seq, LANES), w_alpha_p, b_alpha, gla_norm_gain,
              q_col=rel(offs[4]), k_col=rel(offs[5]), v_col=rel(offs[6]), z_col=rel(offs[7]), group=8)

    out = _out_stage(a_pats, s_pats, rest, ob.reshape(bsz * seq, gv), x2, expand,
                     w_out_attn[0].astype(BF16), w_out_gla[0].astype(BF16), w_out[0].astype(BF16),
                     b_gate, final_norm_gain.reshape(1, d),
                     tm=512, chunks=2, za_col=rel(offs[3]), ga_col=rel(offs[9]), gb_col=rel(offs[10]))
    return out.reshape(bsz, seq, d)
```

```python
import functools

import jax
import jax.numpy as jnp
from jax import lax
from jax.experimental import pallas as pl
from jax.experimental.pallas import tpu as pltpu

ATTN_HEADS = 16
ATTN_HEAD_DIM = 64
ATTN_WINDOW_STEPS = 128
DILATIONS = (1, 4, 16)
GLA_HEADS = 4
GLA_GATE_RANK = 16
GLA_GATE_TAU = 16.0
GLA_CHUNK = 64
NORM_EPS = 1e-6
GROUP_NORM_EPS = 1e-5

LANES = 128
VMEM_LIMIT_BYTES = 56 * 1024 * 1024

MASK_VALUE = -1e30
DEN_LANE = ATTN_HEADS

F32 = jnp.float32
BF16 = jnp.bfloat16


def _dot(a, b):
    return jnp.dot(a, b, preferred_element_type=F32)


def _dot_nt(a, b):
    return lax.dot_general(a, b, (((1,), (1,)), ((), ())), preferred_element_type=F32)


def _dot_tn(a, b):
    return lax.dot_general(a, b, (((0,), (0,)), ((), ())), preferred_element_type=F32)


def _sigmoid(x):
    return 1.0 / (1.0 + jnp.exp(-x))


def _in_proj_kernel(*refs, with_code):
    if with_code:
        x_ref, gain_ref, w_ref, wcode_ref, out_ref, code_ref, u_ref = refs
    else:
        x_ref, gain_ref, w_ref, out_ref, u_ref = refs

    @pl.when(pl.program_id(1) == 0)
    def _():
        x = x_ref[...]
        ms = jnp.mean(x * x, axis=-1, keepdims=True)
        u = (x * lax.rsqrt(ms + NORM_EPS) * gain_ref[...]).astype(BF16)
        u_ref[...] = u
        if with_code:
            code_ref[...] = _dot(u, wcode_ref[...])

    out_ref[...] = _dot(u_ref[...], w_ref[...]).astype(out_ref.dtype)


def _in_proj(x2, gain, w, w_code, *, out_dtype, tm, tn):
    t, d = x2.shape
    n = w.shape[1]
    with_code = w_code is not None
    in_specs = [
        pl.BlockSpec((tm, d), lambda i, j: (i, 0)),
        pl.BlockSpec((1, d), lambda i, j: (0, 0)),
        pl.BlockSpec((d, tn), lambda i, j: (0, j)),
    ]
    out_specs = [pl.BlockSpec((tm, tn), lambda i, j: (i, j))]
    out_shape = [jax.ShapeDtypeStruct((t, n), out_dtype)]
    operands = [x2, gain, w]
    if with_code:
        in_specs.append(pl.BlockSpec((d, LANES), lambda i, j: (0, 0)))
        out_specs.append(pl.BlockSpec((tm, LANES), lambda i, j: (i, 0)))
        out_shape.append(jax.ShapeDtypeStruct((t, LANES), F32))
        operands.append(w_code)
    return pl.pallas_call(
        functools.partial(_in_proj_kernel, with_code=with_code),
        grid=(t // tm, n // tn),
        in_specs=in_specs,
        out_specs=out_specs,
        out_shape=out_shape,
        scratch_shapes=[pltpu.VMEM((tm, d), BF16)],
        compiler_params=pltpu.CompilerParams(
            dimension_semantics=("parallel", "arbitrary"),
            vmem_limit_bytes=VMEM_LIMIT_BYTES),
        name="in_proj_qkv" if with_code else "in_proj_rest",
    )(*operands)


def _gather_classes(src_ref, dst_ref, tmp_ref, dilation, scale):
    seq = src_ref.shape[1]
    length = seq // dilation
    if dilation == 1:
        dst_ref[...] = (src_ref[0] * scale).astype(dst_ref.dtype)
    elif dilation == 4:
        for r in range(4):
            rows = src_ref[0, pl.ds(r, length, stride=4), :]
            dst_ref[r * length:(r + 1) * length, :] = (rows * scale).astype(dst_ref.dtype)
    else:
        assert dilation == 16
        quarter = seq // 4
        for r4 in range(4):
            tmp_ref[r4 * quarter:(r4 + 1) * quarter, :] = src_ref[0, pl.ds(r4, quarter, stride=4), :]
        for r4 in range(4):
            for q4 in range(4):
                r16 = r4 + 4 * q4
                rows = tmp_ref[pl.ds(r4 * quarter + q4, length, stride=4), :]
                dst_ref[r16 * length:(r16 + 1) * length, :] = (rows * scale).astype(dst_ref.dtype)


def _scatter_classes(src_ref, tmp_ref, dilation):
    seq = src_ref.shape[0]
    length = seq // dilation
    if dilation == 1:
        return src_ref
    if dilation == 4:
        for r in range(4):
            tmp_ref[pl.ds(r, length, stride=4), :] = src_ref[r * length:(r + 1) * length, :]
        return tmp_ref
    assert dilation == 16
    quarter = seq // 4
    for r4 in range(4):
        for q4 in range(4):
            r16 = r4 + 4 * q4
            tmp_ref[pl.ds(r4 * quarter + q4, length, stride=4), :] = src_ref[r16 * length:(r16 + 1) * length, :]
    for r4 in range(4):
        src_ref[pl.ds(r4, quarter, stride=4), :] = tmp_ref[r4 * quarter:(r4 + 1) * quarter, :]
    return src_ref


def _attn_kernel(slopes_ref, q_ref, k_ref, v_ref, o_ref, stats_ref,
                 qs_ref, ks_ref, vs_ref, tmp_ref, acc_ref, st_ref, *, dilation, unroll):
    n = ATTN_WINDOW_STEPS
    seq = q_ref.shape[1]
    blocks_per_class = seq // dilation // n
    pair = pl.program_id(1)

    @pl.when(pair == 0)
    def _():
        st_ref[...] = jnp.zeros_like(st_ref)

    _gather_classes(q_ref, qs_ref, tmp_ref, dilation, ATTN_HEAD_DIM ** -0.5)
    _gather_classes(k_ref, ks_ref, tmp_ref, dilation, 1.0)
    _gather_classes(v_ref, vs_ref, tmp_ref, dilation, 1.0)

    row = lax.broadcasted_iota(jnp.int32, (2 * n, 2 * n), 0)
    col = lax.broadcasted_iota(jnp.int32, (2 * n, 2 * n), 1)
    steps = n + (row % n) - col
    band = (steps >= 0) & (steps <= n)
    dist = (steps * dilation).astype(F32)
    slope = jnp.where(row < n, slopes_ref[2 * pair], slopes_ref[2 * pair + 1])
    bias = jnp.where(band, -slope * dist, MASK_VALUE)
    bias_first = jnp.where(col < n, MASK_VALUE, bias)
    lane = lax.broadcasted_iota(jnp.int32, (n, LANES), 1)
    low_half = lane < ATTN_HEAD_DIM
    is_m0, is_m1 = lane == 2 * pair, lane == 2 * pair + 1
    is_d0, is_d1 = lane == DEN_LANE + 2 * pair, lane == DEN_LANE + 2 * pair + 1

    def scores(g):
        r0 = pl.multiple_of(g * n, n)
        rp = pl.multiple_of(jnp.maximum(g - 1, 0) * n, n)
        q = qs_ref[pl.ds(r0, n), :]
        zero = jnp.zeros_like(q)
        qq = jnp.concatenate([jnp.where(low_half, q, zero), jnp.where(low_half, zero, q)], axis=0)
        kk = jnp.concatenate([ks_ref[pl.ds(rp, n), :], ks_ref[pl.ds(r0, n), :]], axis=0)
        first = g % blocks_per_class == 0
        return _dot_nt(qq, kk) + jnp.where(first, bias_first, bias)

    def softmax(s):
        m = jnp.max(s, axis=-1, keepdims=True)
        e = jnp.exp(s - m)
        den = jnp.sum(e, axis=-1, keepdims=True)
        return e.astype(BF16), m, den

    def finish(g, e, m, den):
        r0 = pl.multiple_of(g * n, n)
        rp = pl.multiple_of(jnp.maximum(g - 1, 0) * n, n)
        vv = jnp.concatenate([vs_ref[pl.ds(rp, n), :], vs_ref[pl.ds(r0, n), :]], axis=0)
        pv = _dot(e, vv)
        acc_ref[pl.ds(r0, n), :] = jnp.where(low_half, pv[:n], pv[n:])
        cur = st_ref[pl.ds(r0, n), :]
        cur = jnp.where(is_m0, m[:n], jnp.where(is_m1, m[n:], cur))
        cur = jnp.where(is_d0, den[:n], jnp.where(is_d1, den[n:], cur))
        st_ref[pl.ds(r0, n), :] = cur

    def qblocks(it, carry):
        g0 = it * unroll
        s_next = scores(g0)
        for u in range(unroll):
            s_cur = s_next
            if u + 1 < unroll:
                s_next = scores(g0 + u + 1)
            finish(g0 + u, *softmax(s_cur))
        return carry

    lax.fori_loop(0, seq // n // unroll, qblocks, 0)

    o_ref[0] = _scatter_classes(acc_ref, tmp_ref, dilation)[...].astype(o_ref.dtype)

    @pl.when(pair == pl.num_programs(1) - 1)
    def _():
        stats_ref[0] = _scatter_classes(st_ref, tmp_ref, dilation)[...]


def _attention_pattern(qkv3, slopes, *, dilation, unroll):
    bsz, seq, ncols = qkv3.shape
    width = ncols // 3
    assert seq % (ATTN_WINDOW_STEPS * dilation) == 0
    kb = width // LANES

    def in_map(section):
        return lambda b, s, *_: (b, 0, section * kb + s)

    kernel = functools.partial(_attn_kernel, dilation=dilation, unroll=unroll)
    o, stats = pl.pallas_call(
        kernel,
        grid_spec=pltpu.PrefetchScalarGridSpec(
            num_scalar_prefetch=1,
            grid=(bsz, kb),
            in_specs=[
                pl.BlockSpec((1, seq, LANES), in_map(0)),
                pl.BlockSpec((1, seq, LANES), in_map(1)),
                pl.BlockSpec((1, seq, LANES), in_map(2)),
            ],
            out_specs=[
                pl.BlockSpec((1, seq, LANES), lambda b, s, *_: (b, 0, s)),
                pl.BlockSpec((1, seq, LANES), lambda b, s, *_: (b, 0, 0)),
            ],
            scratch_shapes=[
                pltpu.VMEM((seq, LANES), BF16),
                pltpu.VMEM((seq, LANES), BF16),
                pltpu.VMEM((seq, LANES), BF16),
                pltpu.VMEM((seq, LANES), F32),
                pltpu.VMEM((seq, LANES), F32),
                pltpu.VMEM((seq, LANES), F32),
            ],
        ),
        out_shape=[
            jax.ShapeDtypeStruct((bsz, seq, width), BF16),
            jax.ShapeDtypeStruct((bsz, seq, LANES), F32),
        ],
        compiler_params=pltpu.CompilerParams(
            dimension_semantics=("parallel", "arbitrary"),
            vmem_limit_bytes=VMEM_LIMIT_BYTES),
        name=f"dilated_attn_d{dilation}",
    )(slopes, qkv3, qkv3, qkv3)
    return o.reshape(bsz * seq, width), stats.reshape(bsz * seq, LANES)


def _split_bf16(x):
    hi = x.astype(BF16)
    return hi, (x - hi.astype(F32)).astype(BF16)


def _gla_kernel(q_ref, k_ref, v_ref, z_ref, code_ref, walpha_ref, balpha_ref, gain_ref,
                o_ref, state_ref, b_ref, *, n_chunks, group):
    c = GLA_CHUNK
    dk = q_ref.shape[-1]

    code_hi, code_lo = _split_bf16(code_ref[0])
    w_hi, w_lo = _split_bf16(walpha_ref[...])
    logits = _dot(code_hi, w_hi) + _dot(code_hi, w_lo) + _dot(code_lo, w_hi) + balpha_ref[...]
    log_sig = jnp.minimum(logits, 0.0) - jnp.log(1.0 + jnp.exp(-jnp.abs(logits)))
    b_ref[...] = log_sig / GLA_GATE_TAU

    ri = lax.broadcasted_iota(jnp.int32, (c, c), 0)
    ci = lax.broadcasted_iota(jnp.int32, (c, c), 1)
    causal = ri >= ci
    tri = causal.astype(BF16)
    gain = gain_ref[...]

    def cumsum_chunk(ic, carry):
        rows = pl.ds(pl.multiple_of(ic * c, c), c)
        la = b_ref[rows, :]
        la1 = la.astype(BF16)
        rem = la - la1.astype(F32)
        la2 = rem.astype(BF16)
        la3 = (rem - la2.astype(F32)).astype(BF16)
        b_ref[rows, :] = _dot(tri, la1) + _dot(tri, la2) + _dot(tri, la3)
        return carry

    lax.fori_loop(0, n_chunks, cumsum_chunk, 0, unroll=8)

    state_ref[...] = jnp.zeros_like(state_ref)

    def chunk_group(ig, carry):
        rows = [pl.ds(pl.multiple_of((ig * group + u) * c, c), c) for u in range(group)]
        q_dec, k_inv, k_end, decay, v = [], [], [], [], []
        for u in range(group):
            b = b_ref[rows[u], :]
            b_last = b[c - 1:c, :]
            q = q_ref[0, rows[u], :].astype(F32) * (dk ** -0.5)
            k = k_ref[0, rows[u], :].astype(F32)
            q_dec.append((q * jnp.exp(b)).astype(BF16))
            k_inv.append((k * jnp.exp(-b)).astype(BF16))
            k_end.append((k * jnp.exp(b_last - b)).astype(BF16))
            decay.append(jnp.exp(b_last))
            v.append(v_ref[0, rows[u], :])
        a = [jnp.where(causal, _dot_nt(q_dec[u], k_inv[u]), 0.0).astype(BF16) for u in range(group)]
        d_state = [_dot_tn(v[u], k_end[u]) for u in range(group)]
        o_intra = [_dot(a[u], v[u]) for u in range(group)]
        st = state_ref[...]
        o = []
        for u in range(group):
            o.append(o_intra[u] + _dot_nt(q_dec[u], st.astype(BF16)))
            st = decay[u] * st + d_state[u]
        state_ref[...] = st
        for u in range(group):
            mu = jnp.mean(o[u], axis=-1, keepdims=True)
            cen = o[u] - mu
            var = jnp.mean(cen * cen, axis=-1, keepdims=True)
            y = cen * lax.rsqrt(var + GROUP_NORM_EPS) * gain
            z = z_ref[0, rows[u], :].astype(F32)
            o_ref[0, rows[u], :] = (y * (z * _sigmoid(z))).astype(o_ref.dtype)
        return carry

    lax.fori_loop(0, n_chunks // group, chunk_group, 0)


def _gla(proj3, code3, w_alpha, b_alpha, gain, *, q_col, k_col, v_col, z_col, group):
    bsz, seq, _ = proj3.shape
    dk = w_alpha.shape[1] // GLA_HEADS
    dv = gain.shape[1] // GLA_HEADS
    kernel = functools.partial(_gla_kernel, n_chunks=seq // GLA_CHUNK, group=group)
    return pl.pallas_call(
        kernel,
        grid=(bsz, GLA_HEADS),
        in_specs=[
            pl.BlockSpec((1, seq, dk), lambda b, h: (b, 0, q_col // dk + h)),
            pl.BlockSpec((1, seq, dk), lambda b, h: (b, 0, k_col // dk + h)),
            pl.BlockSpec((1, seq, dv), lambda b, h: (b, 0, v_col // dv + h)),
            pl.BlockSpec((1, seq, dv), lambda b, h: (b, 0, z_col // dv + h)),
            pl.BlockSpec((1, seq, LANES), lambda b, h: (b, 0, 0)),
            pl.BlockSpec((LANES, dk), lambda b, h: (0, h)),
            pl.BlockSpec((1, dk), lambda b, h: (0, h)),
            pl.BlockSpec((1, dv), lambda b, h: (0, h)),
        ],
        out_specs=pl.BlockSpec((1, seq, dv), lambda b, h: (b, 0, h)),
        out_shape=jax.ShapeDtypeStruct((bsz, seq, GLA_HEADS * dv), BF16),
        scratch_shapes=[pltpu.VMEM((dv, dk), F32), pltpu.VMEM((seq, dk), F32)],
        compiler_params=pltpu.CompilerParams(
            dimension_semantics=("parallel", "parallel"),
            vmem_limit_bytes=VMEM_LIMIT_BYTES),
        name="gla",
    )(proj3, proj3, proj3, proj3, code3, w_alpha, b_alpha, gain)


def _out_kernel(a1_ref, a2_ref, a3_ref, s1_ref, s2_ref, s3_ref, za_ref, ob_ref, ga_ref, gb_ref, x_ref,
                expand_ref, wa_ref, wb_ref, wo_ref, bgate_ref, fgain_ref, out_ref, *, chunks):
    tm, d = x_ref.shape
    rc = tm // chunks
    expand = expand_ref[...]
    bg = bgate_ref[...]
    head_lane = lax.broadcasted_iota(jnp.int32, (rc, LANES), 1) < ATTN_HEADS
    ya, yb = [], []
    for c in range(chunks):
        rows = slice(c * rc, (c + 1) * rc)
        stats = [s_ref[rows, :] for s_ref in (s1_ref, s2_ref, s3_ref)]
        m = jnp.maximum(jnp.maximum(stats[0], stats[1]), stats[2])
        scale = [jnp.exp(s - m) for s in stats]
        den = [pltpu.roll(s, LANES - DEN_LANE, 1) for s in stats]
        inv = 1.0 / (scale[0] * den[0] + scale[1] * den[1] + scale[2] * den[2])
        oa = None
        for p, a_ref in enumerate((a1_ref, a2_ref, a3_ref)):
            weight = jnp.where(head_lane, scale[p] * inv, 0.0).astype(BF16)
            term = _dot(weight, expand) * a_ref[rows, :].astype(F32)
            oa = term if oa is None else oa + term
        za = za_ref[rows, :].astype(F32)
        ya.append(_dot((oa * (za * _sigmoid(za))).astype(BF16), wa_ref[...]))
        yb.append(_dot(ob_ref[rows, :], wb_ref[...]))
    h = []
    for c in range(chunks):
        rows = slice(c * rc, (c + 1) * rc)
        gate_a = _sigmoid(ga_ref[rows, :].astype(F32) + bg[:, :d])
        gate_b = _sigmoid(gb_ref[rows, :].astype(F32) + bg[:, d:])
        merged = (gate_a * ya[c] + gate_b * yb[c]).astype(BF16)
        h.append(x_ref[rows, :] + _dot(merged, wo_ref[...]))
    for c in range(chunks):
        rows = slice(c * rc, (c + 1) * rc)
        ms = jnp.mean(h[c] * h[c], axis=-1, keepdims=True)
        out_ref[rows, :] = h[c] * lax.rsqrt(ms + NORM_EPS) * fgain_ref[...]


def _out_stage(a_pats, s_pats, proj2, ob2, x2, expand, wa, wb, wo, bgate, fgain, *,
               tm, chunks, za_col, ga_col, gb_col):
    t, d = x2.shape
    row_blk = lambda cb: pl.BlockSpec((tm, d), lambda i, cb=cb: (i, cb))
    stat_blk = pl.BlockSpec((tm, LANES), lambda i: (i, 0))
    full = lambda a: pl.BlockSpec(a.shape, lambda i: (0, 0))
    return pl.pallas_call(
        functools.partial(_out_kernel, chunks=chunks),
        grid=(t // tm,),
        in_specs=[row_blk(0), row_blk(0), row_blk(0), stat_blk, stat_blk, stat_blk,
                  row_blk(za_col // d), row_blk(0), row_blk(ga_col // d), row_blk(gb_col // d), row_blk(0),
                  full(expand), full(wa), full(wb), full(wo), full(bgate), full(fgain)],
        out_specs=row_blk(0),
        out_shape=jax.ShapeDtypeStruct((t, d), F32),
        compiler_params=pltpu.CompilerParams(
            dimension_semantics=("parallel",),
            vmem_limit_bytes=VMEM_LIMIT_BYTES),
        name="out_stage",
    )(*a_pats, *s_pats, proj2, ob2, proj2, proj2, x2, expand, wa, wb, wo, bgate, fgain)


def kernel(x, norm_gain, w_in, b_gate, w_alpha, b_alpha, gla_norm_gain,
           w_out_attn, w_out_gla, w_out, final_norm_gain):
    bsz, seq, d = x.shape
    assert w_in.shape[0] == 1, "single-layer block"
    aw = ATTN_HEADS * ATTN_HEAD_DIM
    gk = w_alpha.shape[-1]
    gv = gla_norm_gain.shape[-1]
    sizes = (aw, aw, aw, aw, gk, gk, gv, gv, GLA_GATE_RANK, d, d)
    offs = [0]
    for s in sizes:
        offs.append(offs[-1] + s)
    assert offs[-1] == w_in.shape[-1]
    code_lo, code_hi = offs[8], offs[9]
    n_qkv = offs[3]

    w = w_in[0]
    w_main = jnp.concatenate([w[:, :code_lo], w[:, code_hi:]], axis=1).astype(BF16)
    w_code = jnp.pad(w[:, code_lo:code_hi], ((0, 0), (0, LANES - GLA_GATE_RANK))).astype(BF16)
    w_alpha_p = jnp.pad(w_alpha[0], ((0, LANES - GLA_GATE_RANK), (0, 0)))
    slopes = 2.0 ** (-8.0 * (jnp.arange(ATTN_HEADS, dtype=F32) + 1.0) / ATTN_HEADS)
    head_of_col = jnp.arange(aw, dtype=jnp.int32) // ATTN_HEAD_DIM
    expand = (jnp.arange(LANES, dtype=jnp.int32)[:, None] == head_of_col[None, :]).astype(BF16)

    x2 = x.reshape(bsz * seq, d)
    qkv, code = _in_proj(x2, norm_gain, w_main[:, :n_qkv], w_code, out_dtype=F32, tm=2048, tn=768)
    rest, = _in_proj(x2, norm_gain, w_main[:, n_qkv:], None, out_dtype=BF16, tm=2048, tn=1024)
    qkv3 = qkv.reshape(bsz, seq, n_qkv)
    rest3 = rest.reshape(bsz, seq, -1)
    rel = lambda o: o - n_qkv - (GLA_GATE_RANK if o > code_lo else 0)

    pats = [_attention_pattern(qkv3, slopes, dilation=dil, unroll=16) for dil in DILATIONS]
    a_pats = [p[0] for p in pats]
    s_pats = [p[1] for p in pats]

    ob = _gla(rest3, code.reshape(bsz, seq, LANES), w_alpha_p, b_alpha, gla_norm_gain,
              q_col=rel(offs[4]), k_col=rel(offs[5]), v_col=rel(offs[6]), z_col=rel(offs[7]), group=8)

    out = _out_stage(a_pats, s_pats, rest, ob.reshape(bsz * seq, gv), x2, expand,
                     w_out_attn[0].astype(BF16), w_out_gla[0].astype(BF16), w_out[0].astype(BF16),
                     b_gate, final_norm_gain.reshape(1, d),
                     tm=512, chunks=2, za_col=rel(offs[3]), ga_col=rel(offs[9]), gb_col=rel(offs[10]))
    return out.reshape(bsz, seq, d)
```

```python
import functools
import math

import jax
import jax.numpy as jnp
from jax import lax
from jax.experimental import pallas as pl
from jax.experimental.pallas import tpu as pltpu

ATTN_HEADS = 16
ATTN_HEAD_DIM = 64
ATTN_WINDOW_STEPS = 128
DILATIONS = (1, 4, 16)
GLA_HEADS = 4
GLA_GATE_RANK = 16
GLA_GATE_TAU = 16.0
GLA_CHUNK = 64
NORM_EPS = 1e-6
GROUP_NORM_EPS = 1e-5

LANES = 128
VMEM_LIMIT_BYTES = 56 * 1024 * 1024

MASK_VALUE = -1e30
LOG2E = math.log2(math.e)
DEN_LANE = ATTN_HEADS
MOVE_ROWS = 512

F32 = jnp.float32
BF16 = jnp.bfloat16


def _dot(a, b):
    return jnp.dot(a, b, preferred_element_type=F32)


def _dot_nt(a, b):
    return lax.dot_general(a, b, (((1,), (1,)), ((), ())), preferred_element_type=F32)


def _dot_tn(a, b):
    return lax.dot_general(a, b, (((0,), (0,)), ((), ())), preferred_element_type=F32)


def _sigmoid(x):
    return 1.0 / (1.0 + jnp.exp(-x))


def _in_proj_kernel(*refs, with_code):
    if with_code:
        x_ref, gain_ref, w_ref, wcode_ref, out_ref, code_ref, u_ref = refs
    else:
        x_ref, gain_ref, w_ref, out_ref, u_ref = refs

    @pl.when(pl.program_id(1) == 0)
    def _():
        x = x_ref[...]
        ms = jnp.mean(x * x, axis=-1, keepdims=True)
        u = (x * lax.rsqrt(ms + NORM_EPS) * gain_ref[...]).astype(BF16)
        u_ref[...] = u
        if with_code:
            code_ref[...] = _dot(u, wcode_ref[...])

    out_ref[...] = _dot(u_ref[...], w_ref[...]).astype(out_ref.dtype)


def _in_proj(x2, gain, w, w_code, *, out_dtype, tm, tn):
    t, d = x2.shape
    n = w.shape[1]
    with_code = w_code is not None
    in_specs = [
        pl.BlockSpec((tm, d), lambda i, j: (i, 0)),
        pl.BlockSpec((1, d), lambda i, j: (0, 0)),
        pl.BlockSpec((d, tn), lambda i, j: (0, j)),
    ]
    out_specs = [pl.BlockSpec((tm, tn), lambda i, j: (i, j))]
    out_shape = [jax.ShapeDtypeStruct((t, n), out_dtype)]
    operands = [x2, gain, w]
    if with_code:
        in_specs.append(pl.BlockSpec((d, LANES), lambda i, j: (0, 0)))
        out_specs.append(pl.BlockSpec((tm, LANES), lambda i, j: (i, 0)))
        out_shape.append(jax.ShapeDtypeStruct((t, LANES), F32))
        operands.append(w_code)
    return pl.pallas_call(
        functools.partial(_in_proj_kernel, with_code=with_code),
        grid=(t // tm, n // tn),
        in_specs=in_specs,
        out_specs=out_specs,
        out_shape=out_shape,
        scratch_shapes=[pltpu.VMEM((tm, d), BF16)],
        compiler_params=pltpu.CompilerParams(
            dimension_semantics=("parallel", "arbitrary"),
            vmem_limit_bytes=VMEM_LIMIT_BYTES),
        name="in_proj_qkv" if with_code else "in_proj_rest",
    )(*operands)


def _gather_items(src_ref, put, tmp_ref, dilation):
    seq = src_ref.shape[1]
    length = seq // dilation
    items = []
    if dilation == 1:
        for c in range(seq // MOVE_ROWS):
            rows = slice(c * MOVE_ROWS, (c + 1) * MOVE_ROWS)
            items.append(lambda rows=rows: put(rows, src_ref[0, rows, :]))
        return items
    quarter = seq // 4
    pieces = quarter // MOVE_ROWS
    if dilation == 4:
        for r in range(4):
            for c in range(pieces):
                rows = slice(r * quarter + c * MOVE_ROWS, r * quarter + (c + 1) * MOVE_ROWS)
                src = pl.ds(r + 4 * c * MOVE_ROWS, MOVE_ROWS, stride=4)
                items.append(lambda rows=rows, src=src: put(rows, src_ref[0, src, :]))
        return items
    assert dilation == 16

    def to_tmp(rows, src):
        tmp_ref[rows, :] = src_ref[0, src, :]

    for r4 in range(4):
        for c in range(pieces):
            rows = slice(r4 * quarter + c * MOVE_ROWS, r4 * quarter + (c + 1) * MOVE_ROWS)
            src = pl.ds(r4 + 4 * c * MOVE_ROWS, MOVE_ROWS, stride=4)
            items.append(functools.partial(to_tmp, rows, src))
    for r4 in range(4):
        for q4 in range(4):
            r16 = r4 + 4 * q4
            rows = slice(r16 * length, (r16 + 1) * length)
            src = pl.ds(r4 * quarter + q4, length, stride=4)
            items.append(lambda rows=rows, src=src: put(rows, tmp_ref[src, :]))
    return items


def _scatter_items(src_ref, tmp_ref, sink, dilation):
    seq = src_ref.shape[0]
    length = seq // dilation
    quarter = seq // 4
    pieces = quarter // MOVE_ROWS
    items = []

    def move(dst_ref, dst, from_ref, rows):
        dst_ref[dst, :] = from_ref[rows, :]

    final_ref = src_ref
    if dilation == 4:
        for r in range(4):
            for c in range(pieces):
                rows = slice(r * quarter + c * MOVE_ROWS, r * quarter + (c + 1) * MOVE_ROWS)
                dst = pl.ds(r + 4 * c * MOVE_ROWS, MOVE_ROWS, stride=4)
                items.append(functools.partial(move, tmp_ref, dst, src_ref, rows))
        final_ref = tmp_ref
    elif dilation == 16:
        for r4 in range(4):
            for q4 in range(4):
                r16 = r4 + 4 * q4
                rows = slice(r16 * length, (r16 + 1) * length)
                dst = pl.ds(r4 * quarter + q4, length, stride=4)
                items.append(functools.partial(move, tmp_ref, dst, src_ref, rows))
        for r4 in range(4):
            for c in range(pieces):
                rows = slice(r4 * quarter + c * MOVE_ROWS, r4 * quarter + (c + 1) * MOVE_ROWS)
                dst = pl.ds(r4 + 4 * c * MOVE_ROWS, MOVE_ROWS, stride=4)
                items.append(functools.partial(move, src_ref, dst, tmp_ref, rows))
    else:
        assert dilation == 1
    for c in range(seq // MOVE_ROWS):
        rows = slice(c * MOVE_ROWS, (c + 1) * MOVE_ROWS)
        items.append(lambda rows=rows: sink(rows, final_ref[rows, :]))
    return items


def _attn_kernel(slopes_ref, qa_ref, ka_ref, va_ref, qb_ref, kb_ref, vb_ref, o_ref, stats_ref,
                 qs_ref, ks_ref, vs_ref, gtmp_ref, stmp_ref, acc0_ref, acc1_ref, st_ref, *, dilation):
    n = ATTN_WINDOW_STEPS
    seq = qa_ref.shape[1]
    n_blocks = seq // n
    blocks_per_class = seq // dilation // n
    step = pl.program_id(1)
    srcs = ((qa_ref, ka_ref, va_ref), (qb_ref, kb_ref, vb_ref))
    acc_refs = (acc0_ref, acc1_ref)

    @pl.when(step == 0)
    def _():
        st_ref[...] = jnp.zeros_like(st_ref)

    def put_q(group):
        def put(rows, vals):
            q = (vals * (ATTN_HEAD_DIM ** -0.5 * LOG2E)).astype(BF16)
            low = lax.broadcasted_iota(jnp.int32, q.shape, 1) < ATTN_HEAD_DIM
            zero = jnp.zeros_like(q)
            qs_ref[group, 0, rows, :] = jnp.where(low, q, zero)
            qs_ref[group, 1, rows, :] = jnp.where(low, zero, q)
        return put

    def put_plain(dst_ref, group):
        def put(rows, vals):
            dst_ref[group, rows, :] = vals.astype(BF16)
        return put

    def gather(group):
        q_src, k_src, v_src = srcs[group]
        return (_gather_items(q_src, put_q(group), gtmp_ref, dilation)
                + _gather_items(k_src, put_plain(ks_ref, group), gtmp_ref, dilation)
                + _gather_items(v_src, put_plain(vs_ref, group), gtmp_ref, dilation))

    def scatter(group):
        def sink(rows, vals):
            o_ref[0, rows, group * LANES:(group + 1) * LANES] = vals.astype(o_ref.dtype)
        return _scatter_items(acc_refs[group], stmp_ref, sink, dilation)

    row = lax.broadcasted_iota(jnp.int32, (2 * n, 2 * n), 0)
    col = lax.broadcasted_iota(jnp.int32, (2 * n, 2 * n), 1)
    steps = n + (row % n) - col
    band = (steps >= 0) & (steps <= n)
    dist = (steps * dilation).astype(F32) * LOG2E
    lane = lax.broadcasted_iota(jnp.int32, (n, LANES), 1)
    low_half = lane < ATTN_HEAD_DIM

    def run_group(group, side_items):
        pair = 2 * step + group
        slope = jnp.where(row < n, slopes_ref[2 * pair], slopes_ref[2 * pair + 1])
        bias = jnp.where(band, -slope * dist, MASK_VALUE)
        bias_first = jnp.where(col < n, MASK_VALUE, bias)
        is_m0, is_m1 = lane == 2 * pair, lane == 2 * pair + 1
        is_d0, is_d1 = lane == DEN_LANE + 2 * pair, lane == DEN_LANE + 2 * pair + 1
        acc_ref = acc_refs[group]

        def block_rows(g):
            return slice(g * n, (g + 1) * n), slice(max(g - 1, 0) * n, max(g, 1) * n)

        def scores(g):
            rows, prev = block_rows(g)
            qq = jnp.concatenate([qs_ref[group, 0, rows, :], qs_ref[group, 1, rows, :]], axis=0)
            kk = jnp.concatenate([ks_ref[group, prev, :], ks_ref[group, rows, :]], axis=0)
            first = g % blocks_per_class == 0
            return _dot_nt(qq, kk) + (bias_first if first else bias)

        def finish(g, s):
            rows, prev = block_rows(g)
            m = jnp.max(s, axis=-1, keepdims=True)
            e = jnp.exp2(s - m)
            den = jnp.sum(e, axis=-1, keepdims=True)
            vv = jnp.concatenate([vs_ref[group, prev, :], vs_ref[group, rows, :]], axis=0)
            pv = _dot(e.astype(BF16), vv)
            acc_ref[rows, :] = jnp.where(low_half, pv[:n], pv[n:])
            cur = st_ref[rows, :]
            cur = jnp.where(is_m0, m[:n], jnp.where(is_m1, m[n:], cur))
            cur = jnp.where(is_d0, den[:n], jnp.where(is_d1, den[n:], cur))
            st_ref[rows, :] = cur

        s_next = scores(0)
        for g in range(n_blocks):
            s_cur = s_next
            if g + 1 < n_blocks:
                s_next = scores(g + 1)
            finish(g, s_cur)
            for item in side_items[g * len(side_items) // n_blocks:(g + 1) * len(side_items) // n_blocks]:
                item()

    for item in gather(0):
        item()
    run_group(0, gather(1))
    run_group(1, scatter(0))
    for item in scatter(1):
        item()

    @pl.when(step == pl.num_programs(1) - 1)
    def _():
        def sink(rows, vals):
            stats_ref[0, rows, :] = vals
        for item in _scatter_items(st_ref, stmp_ref, sink, dilation):
            item()


def _attention_pattern(qkv3, slopes, *, dilation):
    bsz, seq, ncols = qkv3.shape
    width = ncols // 3
    assert seq % (ATTN_WINDOW_STEPS * dilation) == 0 and seq % (4 * MOVE_ROWS) == 0
    kb = width // LANES

    def in_map(section, group):
        return lambda b, s, *_: (b, 0, section * kb + 2 * s + group)

    kernel = functools.partial(_attn_kernel, dilation=dilation)
    tile = lambda dtype: pltpu.VMEM((seq, LANES), dtype)
    o, stats = pl.pallas_call(
        kernel,
        grid_spec=pltpu.PrefetchScalarGridSpec(
            num_scalar_prefetch=1,
            grid=(bsz, kb // 2),
            in_specs=[pl.BlockSpec((1, seq, LANES), in_map(section, group))
                      for group in range(2) for section in range(3)],
            out_specs=[
                pl.BlockSpec((1, seq, 2 * LANES), lambda b, s, *_: (b, 0, s)),
                pl.BlockSpec((1, seq, LANES), lambda b, s, *_: (b, 0, 0)),
            ],
            scratch_shapes=[
                pltpu.VMEM((2, 2, seq, LANES), BF16),
                pltpu.VMEM((2, seq, LANES), BF16),
                pltpu.VMEM((2, seq, LANES), BF16),
                tile(F32),
                tile(F32),
                tile(F32),
                tile(F32),
                tile(F32),
            ],
        ),
        out_shape=[
            jax.ShapeDtypeStruct((bsz, seq, width), BF16),
            jax.ShapeDtypeStruct((bsz, seq, LANES), F32),
        ],
        compiler_params=pltpu.CompilerParams(
            dimension_semantics=("parallel", "arbitrary"),
            vmem_limit_bytes=VMEM_LIMIT_BYTES),
        name=f"dilated_attn_d{dilation}",
    )(slopes, *([qkv3] * 6))
    return o.reshape(bsz * seq, width), stats.reshape(bsz * seq, LANES)


def _split_bf16(x):
    hi = x.astype(BF16)
    return hi, (x - hi.astype(F32)).astype(BF16)


def _gla_kernel(q_ref, k_ref, v_ref, z_ref, code_ref, walpha_ref, balpha_ref, gain_ref,
                o_ref, state_ref, b_ref, *, n_chunks, group):
    c = GLA_CHUNK
    dk = q_ref.shape[-1]

    code_hi, code_lo = _split_bf16(code_ref[0])
    w_hi, w_lo = _split_bf16(walpha_ref[...])
    logits = _dot(code_hi, w_hi) + _dot(code_hi, w_lo) + _dot(code_lo, w_hi) + balpha_ref[...]
    log_sig = jnp.minimum(logits, 0.0) - jnp.log(1.0 + jnp.exp(-jnp.abs(logits)))
    b_ref[...] = log_sig / GLA_GATE_TAU

    ri = lax.broadcasted_iota(jnp.int32, (c, c), 0)
    ci = lax.broadcasted_iota(jnp.int32, (c, c), 1)
    causal = ri >= ci
    tri = causal.astype(BF16)
    gain = gain_ref[...]

    def cumsum_chunk(ic, carry):
        rows = pl.ds(pl.multiple_of(ic * c, c), c)
        la = b_ref[rows, :]
        la1 = la.astype(BF16)
        rem = la - la1.astype(F32)
        la2 = rem.astype(BF16)
        la3 = (rem - la2.astype(F32)).astype(BF16)
        b_ref[rows, :] = _dot(tri, la1) + _dot(tri, la2) + _dot(tri, la3)
        return carry

    lax.fori_loop(0, n_chunks, cumsum_chunk, 0, unroll=8)

    state_ref[...] = jnp.zeros_like(state_ref)

    def chunk_group(ig, carry):
        rows = [pl.ds(pl.multiple_of((ig * group + u) * c, c), c) for u in range(group)]
        q_dec, k_inv, k_end, decay, v = [], [], [], [], []
        for u in range(group):
            b = b_ref[rows[u], :]
            b_last = b[c - 1:c, :]
            q = q_ref[0, rows[u], :].astype(F32) * (dk ** -0.5)
            k = k_ref[0, rows[u], :].astype(F32)
            q_dec.append((q * jnp.exp(b)).astype(BF16))
            k_inv.append((k * jnp.exp(-b)).astype(BF16))
            k_end.append((k * jnp.exp(b_last - b)).astype(BF16))
            decay.append(jnp.exp(b_last))
            v.append(v_ref[0, rows[u], :])
        a = [jnp.where(causal, _dot_nt(q_dec[u], k_inv[u]), 0.0).astype(BF16) for u in range(group)]
        d_state = [_dot_tn(v[u], k_end[u]) for u in range(group)]
        o_intra = [_dot(a[u], v[u]) for u in range(group)]
        st = state_ref[...]
        o = []
        for u in range(group):
            o.append(o_intra[u] + _dot_nt(q_dec[u], st.astype(BF16)))
            st = decay[u] * st + d_state[u]
        state_ref[...] = st
        for u in range(group):
            mu = jnp.mean(o[u], axis=-1, keepdims=True)
            cen = o[u] - mu
            var = jnp.mean(cen * cen, axis=-1, keepdims=True)
            y = cen * lax.rsqrt(var + GROUP_NORM_EPS) * gain
            z = z_ref[0, rows[u], :].astype(F32)
            o_ref[0, rows[u], :] = (y * (z * _sigmoid(z))).astype(o_ref.dtype)
        return carry

    lax.fori_loop(0, n_chunks // group, chunk_group, 0)


def _gla(proj3, code3, w_alpha, b_alpha, gain, *, q_col, k_col, v_col, z_col, group):
    bsz, seq, _ = proj3.shape
    dk = w_alpha.shape[1] // GLA_HEADS
    dv = gain.shape[1] // GLA_HEADS
    kernel = functools.partial(_gla_kernel, n_chunks=seq // GLA_CHUNK, group=group)
    return pl.pallas_call(
        kernel,
        grid=(bsz, GLA_HEADS),
        in_specs=[
            pl.BlockSpec((1, seq, dk), lambda b, h: (b, 0, q_col // dk + h)),
            pl.BlockSpec((1, seq, dk), lambda b, h: (b, 0, k_col // dk + h)),
            pl.BlockSpec((1, seq, dv), lambda b, h: (b, 0, v_col // dv + h)),
            pl.BlockSpec((1, seq, dv), lambda b, h: (b, 0, z_col // dv + h)),
            pl.BlockSpec((1, seq, LANES), lambda b, h: (b, 0, 0)),
            pl.BlockSpec((LANES, dk), lambda b, h: (0, h)),
            pl.BlockSpec((1, dk), lambda b, h: (0, h)),
            pl.BlockSpec((1, dv), lambda b, h: (0, h)),
        ],
        out_specs=pl.BlockSpec((1, seq, dv), lambda b, h: (b, 0, h)),
        out_shape=jax.ShapeDtypeStruct((bsz, seq, GLA_HEADS * dv), BF16),
        scratch_shapes=[pltpu.VMEM((dv, dk), F32), pltpu.VMEM((seq, dk), F32)],
        compiler_params=pltpu.CompilerParams(
            dimension_semantics=("parallel", "parallel"),
            vmem_limit_bytes=VMEM_LIMIT_BYTES),
        name="gla",
    )(proj3, proj3, proj3, proj3, code3, w_alpha, b_alpha, gain)


def _out_kernel(a1_ref, a2_ref, a3_ref, s1_ref, s2_ref, s3_ref, za_ref, ob_ref, ga_ref, gb_ref, x_ref,
                expand_ref, wa_ref, wb_ref, wo_ref, bgate_ref, fgain_ref, out_ref, *, chunks):
    tm, d = x_ref.shape
    rc = tm // chunks
    expand = expand_ref[...]
    bg = bgate_ref[...]
    head_lane = lax.broadcasted_iota(jnp.int32, (rc, LANES), 1) < ATTN_HEADS
    ya, yb = [], []
    for c in range(chunks):
        rows = slice(c * rc, (c + 1) * rc)
        stats = [s_ref[rows, :] for s_ref in (s1_ref, s2_ref, s3_ref)]
        m = jnp.maximum(jnp.maximum(stats[0], stats[1]), stats[2])
        scale = [jnp.exp2(s - m) for s in stats]
        den = [pltpu.roll(s, LANES - DEN_LANE, 1) for s in stats]
        inv = 1.0 / (scale[0] * den[0] + scale[1] * den[1] + scale[2] * den[2])
        oa = None
        for p, a_ref in enumerate((a1_ref, a2_ref, a3_ref)):
            weight = jnp.where(head_lane, scale[p] * inv, 0.0).astype(BF16)
            term = _dot(weight, expand) * a_ref[rows, :].astype(F32)
            oa = term if oa is None else oa + term
        za = za_ref[rows, :].astype(F32)
        ya.append(_dot((oa * (za * _sigmoid(za))).astype(BF16), wa_ref[...]))
        yb.append(_dot(ob_ref[rows, :], wb_ref[...]))
    h = []
    for c in range(chunks):
        rows = slice(c * rc, (c + 1) * rc)
        gate_a = _sigmoid(ga_ref[rows, :].astype(F32) + bg[:, :d])
        gate_b = _sigmoid(gb_ref[rows, :].astype(F32) + bg[:, d:])
        merged = (gate_a * ya[c] + gate_b * yb[c]).astype(BF16)
        h.append(x_ref[rows, :] + _dot(merged, wo_ref[...]))
    for c in range(chunks):
        rows = slice(c * rc, (c + 1) * rc)
        ms = jnp.mean(h[c] * h[c], axis=-1, keepdims=True)
        out_ref[rows, :] = h[c] * lax.rsqrt(ms + NORM_EPS) * fgain_ref[...]


def _out_stage(a_pats, s_pats, proj2, ob2, x2, expand, wa, wb, wo, bgate, fgain, *,
               tm, chunks, za_col, ga_col, gb_col):
    t, d = x2.shape
    row_blk = lambda cb: pl.BlockSpec((tm, d), lambda i, cb=cb: (i, cb))
    stat_blk = pl.BlockSpec((tm, LANES), lambda i: (i, 0))
    full = lambda a: pl.BlockSpec(a.shape, lambda i: (0, 0))
    return pl.pallas_call(
        functools.partial(_out_kernel, chunks=chunks),
        grid=(t // tm,),
        in_specs=[row_blk(0), row_blk(0), row_blk(0), stat_blk, stat_blk, stat_blk,
                  row_blk(za_col // d), row_blk(0), row_blk(ga_col // d), row_blk(gb_col // d), row_blk(0),
                  full(expand), full(wa), full(wb), full(wo), full(bgate), full(fgain)],
        out_specs=row_blk(0),
        out_shape=jax.ShapeDtypeStruct((t, d), F32),
        compiler_params=pltpu.CompilerParams(
            dimension_semantics=("parallel",),
            vmem_limit_bytes=VMEM_LIMIT_BYTES),
        name="out_stage",
    )(*a_pats, *s_pats, proj2, ob2, proj2, proj2, x2, expand, wa, wb, wo, bgate, fgain)


def kernel(x, norm_gain, w_in, b_gate, w_alpha, b_alpha, gla_norm_gain,
           w_out_attn, w_out_gla, w_out, final_norm_gain):
    bsz, seq, d = x.shape
    assert w_in.shape[0] == 1, "single-layer block"
    aw = ATTN_HEADS * ATTN_HEAD_DIM
    gk = w_alpha.shape[-1]
    gv = gla_norm_gain.shape[-1]
    sizes = (aw, aw, aw, aw, gk, gk, gv, gv, GLA_GATE_RANK, d, d)
    offs = [0]
    for s in sizes:
        offs.append(offs[-1] + s)
    assert offs[-1] == w_in.shape[-1]
    code_lo, code_hi = offs[8], offs[9]
    n_qkv = offs[3]

    w = w_in[0]
    w_main = jnp.concatenate([w[:, :code_lo], w[:, code_hi:]], axis=1).astype(BF16)
    w_code = jnp.pad(w[:, code_lo:code_hi], ((0, 0), (0, LANES - GLA_GATE_RANK))).astype(BF16)
    w_alpha_p = jnp.pad(w_alpha[0], ((0, LANES - GLA_GATE_RANK), (0, 0)))
    slopes = 2.0 ** (-8.0 * (jnp.arange(ATTN_HEADS, dtype=F32) + 1.0) / ATTN_HEADS)
    head_of_col = jnp.arange(aw, dtype=jnp.int32) // ATTN_HEAD_DIM
    expand = (jnp.arange(LANES, dtype=jnp.int32)[:, None] == head_of_col[None, :]).astype(BF16)

    x2 = x.reshape(bsz * seq, d)
    qkv, code = _in_proj(x2, norm_gain, w_main[:, :n_qkv], w_code, out_dtype=F32, tm=2048, tn=768)
    rest, = _in_proj(x2, norm_gain, w_main[:, n_qkv:], None, out_dtype=BF16, tm=2048, tn=1024)
    qkv3 = qkv.reshape(bsz, seq, n_qkv)
    rest3 = rest.reshape(bsz, seq, -1)
    rel = lambda o: o - n_qkv - (GLA_GATE_RANK if o > code_lo else 0)

    pats = [_attention_pattern(qkv3, slopes, dilation=dil) for dil in DILATIONS]
    a_pats = [p[0] for p in pats]
    s_pats = [p[1] for p in pats]

    ob = _gla(rest3, code.reshape(bsz, seq, LANES), w_alpha_p, b_alpha, gla_norm_gain,
              q_col=rel(offs[4]), k_col=rel(offs[5]), v_col=rel(offs[6]), z_col=rel(offs[7]), group=8)

    out = _out_stage(a_pats, s_pats, rest, ob.reshape(bsz * seq, gv), x2, expand,
                     w_out_attn[0].astype(BF16), w_out_gla[0].astype(BF16), w_out[0].astype(BF16),
                     b_gate, final_norm_gain.reshape(1, d),
                     tm=512, chunks=2, za_col=rel(offs[3]), ga_col=rel(offs[9]), gb_col=rel(offs[10]))
    return out.reshape(bsz, seq, d)
```

```python
import functools
import math

import jax
import jax.numpy as jnp
from jax import lax
from jax.experimental import pallas as pl
from jax.experimental.pallas import tpu as pltpu

ATTN_HEADS = 16
ATTN_HEAD_DIM = 64
ATTN_WINDOW_STEPS = 128
DILATIONS = (1, 4, 16)
GLA_HEADS = 4
GLA_GATE_RANK = 16
GLA_GATE_TAU = 16.0
GLA_CHUNK = 64
NORM_EPS = 1e-6
GROUP_NORM_EPS = 1e-5

LANES = 128
VMEM_LIMIT_BYTES = 56 * 1024 * 1024

MASK_VALUE = -1e30
LOG2E = math.log2(math.e)
DEN_LANE = ATTN_HEADS
MOVE_ROWS = 512

F32 = jnp.float32
BF16 = jnp.bfloat16


def _dot(a, b):
    return jnp.dot(a, b, preferred_element_type=F32)


def _dot_nt(a, b):
    return lax.dot_general(a, b, (((1,), (1,)), ((), ())), preferred_element_type=F32)


def _dot_tn(a, b):
    return lax.dot_general(a, b, (((0,), (0,)), ((), ())), preferred_element_type=F32)


def _sigmoid(x):
    return 1.0 / (1.0 + jnp.exp(-x))


def _in_proj_kernel(*refs, with_code):
    if with_code:
        x_ref, gain_ref, w_ref, wcode_ref, out_ref, code_ref, u_ref = refs
    else:
        x_ref, gain_ref, w_ref, out_ref, u_ref = refs

    @pl.when(pl.program_id(1) == 0)
    def _():
        x = x_ref[...]
        ms = jnp.mean(x * x, axis=-1, keepdims=True)
        u = (x * lax.rsqrt(ms + NORM_EPS) * gain_ref[...]).astype(BF16)
        u_ref[...] = u
        if with_code:
            code_ref[...] = _dot(u, wcode_ref[...])

    out_ref[...] = _dot(u_ref[...], w_ref[...]).astype(out_ref.dtype)


def _in_proj(x2, gain, w, w_code, *, out_dtype, tm, tn):
    t, d = x2.shape
    n = w.shape[1]
    with_code = w_code is not None
    in_specs = [
        pl.BlockSpec((tm, d), lambda i, j: (i, 0)),
        pl.BlockSpec((1, d), lambda i, j: (0, 0)),
        pl.BlockSpec((d, tn), lambda i, j: (0, j)),
    ]
    out_specs = [pl.BlockSpec((tm, tn), lambda i, j: (i, j))]
    out_shape = [jax.ShapeDtypeStruct((t, n), out_dtype)]
    operands = [x2, gain, w]
    if with_code:
        in_specs.append(pl.BlockSpec((d, LANES), lambda i, j: (0, 0)))
        out_specs.append(pl.BlockSpec((tm, LANES), lambda i, j: (i, 0)))
        out_shape.append(jax.ShapeDtypeStruct((t, LANES), F32))
        operands.append(w_code)
    return pl.pallas_call(
        functools.partial(_in_proj_kernel, with_code=with_code),
        grid=(t // tm, n // tn),
        in_specs=in_specs,
        out_specs=out_specs,
        out_shape=out_shape,
        scratch_shapes=[pltpu.VMEM((tm, d), BF16)],
        compiler_params=pltpu.CompilerParams(
            dimension_semantics=("parallel", "arbitrary"),
            vmem_limit_bytes=VMEM_LIMIT_BYTES),
        name="in_proj_qkv" if with_code else "in_proj_rest",
    )(*operands)


def _gather_items(src_ref, put, tmp_ref, dilation):
    seq = src_ref.shape[1]
    length = seq // dilation
    items = []
    if dilation == 1:
        for c in range(seq // MOVE_ROWS):
            rows = slice(c * MOVE_ROWS, (c + 1) * MOVE_ROWS)
            items.append(lambda rows=rows: put(rows, src_ref[0, rows, :]))
        return items
    quarter = seq // 4
    pieces = quarter // MOVE_ROWS
    if dilation == 4:
        for r in range(4):
            for c in range(pieces):
                rows = slice(r * quarter + c * MOVE_ROWS, r * quarter + (c + 1) * MOVE_ROWS)
                src = pl.ds(r + 4 * c * MOVE_ROWS, MOVE_ROWS, stride=4)
                items.append(lambda rows=rows, src=src: put(rows, src_ref[0, src, :]))
        return items
    assert dilation == 16

    def to_tmp(rows, src):
        tmp_ref[rows, :] = src_ref[0, src, :]

    for r4 in range(4):
        for c in range(pieces):
            rows = slice(r4 * quarter + c * MOVE_ROWS, r4 * quarter + (c + 1) * MOVE_ROWS)
            src = pl.ds(r4 + 4 * c * MOVE_ROWS, MOVE_ROWS, stride=4)
            items.append(functools.partial(to_tmp, rows, src))
    for r4 in range(4):
        for q4 in range(4):
            r16 = r4 + 4 * q4
            rows = slice(r16 * length, (r16 + 1) * length)
            src = pl.ds(r4 * quarter + q4, length, stride=4)
            items.append(lambda rows=rows, src=src: put(rows, tmp_ref[src, :]))
    return items


def _scatter_items(src_ref, tmp_ref, sink, dilation):
    seq = src_ref.shape[0]
    length = seq // dilation
    quarter = seq // 4
    pieces = quarter // MOVE_ROWS
    items = []

    def move(dst_ref, dst, from_ref, rows):
        dst_ref[dst, :] = from_ref[rows, :]

    final_ref = src_ref
    if dilation == 4:
        for r in range(4):
            for c in range(pieces):
                rows = slice(r * quarter + c * MOVE_ROWS, r * quarter + (c + 1) * MOVE_ROWS)
                dst = pl.ds(r + 4 * c * MOVE_ROWS, MOVE_ROWS, stride=4)
                items.append(functools.partial(move, tmp_ref, dst, src_ref, rows))
        final_ref = tmp_ref
    elif dilation == 16:
        for r4 in range(4):
            for q4 in range(4):
                r16 = r4 + 4 * q4
                rows = slice(r16 * length, (r16 + 1) * length)
                dst = pl.ds(r4 * quarter + q4, length, stride=4)
                items.append(functools.partial(move, tmp_ref, dst, src_ref, rows))
        for r4 in range(4):
            for c in range(pieces):
                rows = slice(r4 * quarter + c * MOVE_ROWS, r4 * quarter + (c + 1) * MOVE_ROWS)
                dst = pl.ds(r4 + 4 * c * MOVE_ROWS, MOVE_ROWS, stride=4)
                items.append(functools.partial(move, src_ref, dst, tmp_ref, rows))
    else:
        assert dilation == 1
    for c in range(seq // MOVE_ROWS):
        rows = slice(c * MOVE_ROWS, (c + 1) * MOVE_ROWS)
        items.append(lambda rows=rows: sink(rows, final_ref[rows, :]))
    return items


def _attn_kernel(slopes_ref, qa_ref, ka_ref, va_ref, qb_ref, kb_ref, vb_ref, o_ref, stats_ref,
                 qs_ref, ks_ref, vs_ref, gtmp_ref, stmp_ref, acc0_ref, acc1_ref, st_ref, *, dilation):
    n = ATTN_WINDOW_STEPS
    seq = qa_ref.shape[1]
    n_blocks = seq // n
    blocks_per_class = seq // dilation // n
    step = pl.program_id(1)
    srcs = ((qa_ref, ka_ref, va_ref), (qb_ref, kb_ref, vb_ref))
    acc_refs = (acc0_ref, acc1_ref)

    @pl.when(step == 0)
    def _():
        st_ref[...] = jnp.zeros_like(st_ref)

    def put_q(group):
        def put(rows, vals):
            q = (vals * (ATTN_HEAD_DIM ** -0.5 * LOG2E)).astype(BF16)
            low = lax.broadcasted_iota(jnp.int32, q.shape, 1) < ATTN_HEAD_DIM
            zero = jnp.zeros_like(q)
            qs_ref[group, 0, rows, :] = jnp.where(low, q, zero)
            qs_ref[group, 1, rows, :] = jnp.where(low, zero, q)
        return put

    def put_plain(dst_ref, group):
        def put(rows, vals):
            dst_ref[group, rows, :] = vals.astype(BF16)
        return put

    def gather(group):
        q_src, k_src, v_src = srcs[group]
        return (_gather_items(q_src, put_q(group), gtmp_ref, dilation)
                + _gather_items(k_src, put_plain(ks_ref, group), gtmp_ref, dilation)
                + _gather_items(v_src, put_plain(vs_ref, group), gtmp_ref, dilation))

    def scatter(group):
        def sink(rows, vals):
            o_ref[0, rows, group * LANES:(group + 1) * LANES] = vals.astype(o_ref.dtype)
        return _scatter_items(acc_refs[group], stmp_ref, sink, dilation)

    row = lax.broadcasted_iota(jnp.int32, (2 * n, 2 * n), 0)
    col = lax.broadcasted_iota(jnp.int32, (2 * n, 2 * n), 1)
    steps = n + (row % n) - col
    band = (steps >= 0) & (steps <= n)
    dist = (steps * dilation).astype(F32) * LOG2E
    lane = lax.broadcasted_iota(jnp.int32, (n, LANES), 1)
    low_half = lane < ATTN_HEAD_DIM

    def run_group(group, side_items):
        pair = 2 * step + group
        slope = jnp.where(row < n, slopes_ref[2 * pair], slopes_ref[2 * pair + 1])
        bias = jnp.where(band, -slope * dist, MASK_VALUE)
        bias_first = jnp.where(col < n, MASK_VALUE, bias)
        is_m0, is_m1 = lane == 2 * pair, lane == 2 * pair + 1
        is_d0, is_d1 = lane == DEN_LANE + 2 * pair, lane == DEN_LANE + 2 * pair + 1
        acc_ref = acc_refs[group]

        def block_rows(g):
            return slice(g * n, (g + 1) * n), slice(max(g - 1, 0) * n, max(g, 1) * n)

        def scores(g):
            rows, prev = block_rows(g)
            qq = jnp.concatenate([qs_ref[group, 0, rows, :], qs_ref[group, 1, rows, :]], axis=0)
            kk = jnp.concatenate([ks_ref[group, prev, :], ks_ref[group, rows, :]], axis=0)
            first = g % blocks_per_class == 0
            return _dot_nt(qq, kk) + (bias_first if first else bias)

        def finish(g, s):
            rows, prev = block_rows(g)
            m = jnp.max(s, axis=-1, keepdims=True)
            e = jnp.exp2(s - m)
            den = jnp.sum(e, axis=-1, keepdims=True)
            vv = jnp.concatenate([vs_ref[group, prev, :], vs_ref[group, rows, :]], axis=0)
            pv = _dot(e.astype(BF16), vv)
            acc_ref[rows, :] = jnp.where(low_half, pv[:n], pv[n:])
            cur = st_ref[rows, :]
            cur = jnp.where(is_m0, m[:n], jnp.where(is_m1, m[n:], cur))
            cur = jnp.where(is_d0, den[:n], jnp.where(is_d1, den[n:], cur))
            st_ref[rows, :] = cur

        s_next = scores(0)
        for g in range(n_blocks):
            s_cur = s_next
            if g + 1 < n_blocks:
                s_next = scores(g + 1)
            finish(g, s_cur)
            for item in side_items[g * len(side_items) // n_blocks:(g + 1) * len(side_items) // n_blocks]:
                item()

    for item in gather(0):
        item()
    run_group(0, gather(1))
    run_group(1, scatter(0))
    for item in scatter(1):
        item()

    @pl.when(step == pl.num_programs(1) - 1)
    def _():
        def sink(rows, vals):
            stats_ref[0, rows, :] = vals
        for item in _scatter_items(st_ref, stmp_ref, sink, dilation):
            item()


def _attention_pattern(qkv3, slopes, *, dilation):
    bsz, seq, ncols = qkv3.shape
    width = ncols // 3
    assert seq % (ATTN_WINDOW_STEPS * dilation) == 0 and seq % (4 * MOVE_ROWS) == 0
    kb = width // LANES

    def in_map(section, group):
        return lambda b, s, *_: (b, 0, section * kb + 2 * s + group)

    kernel = functools.partial(_attn_kernel, dilation=dilation)
    tile = lambda dtype: pltpu.VMEM((seq, LANES), dtype)
    o, stats = pl.pallas_call(
        kernel,
        grid_spec=pltpu.PrefetchScalarGridSpec(
            num_scalar_prefetch=1,
            grid=(bsz, kb // 2),
            in_specs=[pl.BlockSpec((1, seq, LANES), in_map(section, group))
                      for group in range(2) for section in range(3)],
            out_specs=[
                pl.BlockSpec((1, seq, 2 * LANES), lambda b, s, *_: (b, 0, s)),
                pl.BlockSpec((1, seq, LANES), lambda b, s, *_: (b, 0, 0)),
            ],
            scratch_shapes=[
                pltpu.VMEM((2, 2, seq, LANES), BF16),
                pltpu.VMEM((2, seq, LANES), BF16),
                pltpu.VMEM((2, seq, LANES), BF16),
                tile(F32),
                tile(F32),
                tile(F32),
                tile(F32),
                tile(F32),
            ],
        ),
        out_shape=[
            jax.ShapeDtypeStruct((bsz, seq, width), BF16),
            jax.ShapeDtypeStruct((bsz, seq, LANES), F32),
        ],
        compiler_params=pltpu.CompilerParams(
            dimension_semantics=("parallel", "arbitrary"),
            vmem_limit_bytes=VMEM_LIMIT_BYTES),
        name=f"dilated_attn_d{dilation}",
    )(slopes, *([qkv3] * 6))
    return o.reshape(bsz * seq, width), stats.reshape(bsz * seq, LANES)


def _split_bf16(x):
    hi = x.astype(BF16)
    return hi, (x - hi.astype(F32)).astype(BF16)


def _gla_kernel(q_ref, k_ref, v_ref, z_ref, code_ref, walpha_ref, balpha_ref, gain_ref,
                o_ref, b_ref, *, group):
    c = GLA_CHUNK
    seq, dk = q_ref.shape[1], q_ref.shape[2]
    dv = v_ref.shape[2]
    n_groups = seq // (c * group)

    ri = lax.broadcasted_iota(jnp.int32, (c, c), 0)
    ci = lax.broadcasted_iota(jnp.int32, (c, c), 1)
    causal = ri >= ci
    tri = causal.astype(BF16)
    gain = gain_ref[...]
    w_hi, w_lo = _split_bf16(walpha_ref[...])
    b_alpha = balpha_ref[...]

    def chunk_rows(g, u):
        return slice((g * group + u) * c, (g * group + u + 1) * c)

    def gate_logs(g):
        rows = slice(g * group * c, (g + 1) * group * c)
        code_hi, code_lo = _split_bf16(code_ref[0, rows, :])
        logits = _dot(code_hi, w_hi) + _dot(code_hi, w_lo) + _dot(code_lo, w_hi) + b_alpha
        log_sig = jnp.minimum(logits, 0.0) - jnp.log(1.0 + jnp.exp(-jnp.abs(logits)))
        b_ref[rows, :] = log_sig / GLA_GATE_TAU

    def cumsums(g):
        for u in range(group):
            rows = chunk_rows(g, u)
            la = b_ref[rows, :]
            la1 = la.astype(BF16)
            rem = la - la1.astype(F32)
            la2 = rem.astype(BF16)
            la3 = (rem - la2.astype(F32)).astype(BF16)
            b_ref[rows, :] = _dot(tri, la1) + _dot(tri, la2) + _dot(tri, la3)

    gate_logs(0)
    cumsums(0)
    st = jnp.zeros((dv, dk), F32)
    for g in range(n_groups):
        rows = [chunk_rows(g, u) for u in range(group)]
        q_dec, k_inv, k_end, decay, v = [], [], [], [], []
        for u in range(group):
            b = b_ref[rows[u], :]
            b_last = b[c - 1:c, :]
            q = q_ref[0, rows[u], :].astype(F32) * (dk ** -0.5)
            k = k_ref[0, rows[u], :].astype(F32)
            q_dec.append((q * jnp.exp(b)).astype(BF16))
            k_inv.append((k * jnp.exp(-b)).astype(BF16))
            k_end.append((k * jnp.exp(b_last - b)).astype(BF16))
            decay.append(jnp.exp(b_last))
            v.append(v_ref[0, rows[u], :])
        a = [jnp.where(causal, _dot_nt(q_dec[u], k_inv[u]), 0.0).astype(BF16) for u in range(group)]
        d_state = [_dot_tn(v[u], k_end[u]) for u in range(group)]
        o_intra = [_dot(a[u], v[u]) for u in range(group)]
        if g + 1 < n_groups:
            gate_logs(g + 1)
        o = []
        for u in range(group):
            o.append(o_intra[u] + _dot_nt(q_dec[u], st.astype(BF16)))
            st = decay[u] * st + d_state[u]
        if g + 1 < n_groups:
            cumsums(g + 1)
        for u in range(group):
            mu = jnp.mean(o[u], axis=-1, keepdims=True)
            cen = o[u] - mu
            var = jnp.mean(cen * cen, axis=-1, keepdims=True)
            y = cen * lax.rsqrt(var + GROUP_NORM_EPS) * gain
            z = z_ref[0, rows[u], :].astype(F32)
            o_ref[0, rows[u], :] = (y * (z * _sigmoid(z))).astype(o_ref.dtype)


def _gla(proj3, code3, w_alpha, b_alpha, gain, *, q_col, k_col, v_col, z_col, group):
    bsz, seq, _ = proj3.shape
    dk = w_alpha.shape[1] // GLA_HEADS
    dv = gain.shape[1] // GLA_HEADS
    assert seq % (GLA_CHUNK * group) == 0
    kernel = functools.partial(_gla_kernel, group=group)
    return pl.pallas_call(
        kernel,
        grid=(bsz, GLA_HEADS),
        in_specs=[
            pl.BlockSpec((1, seq, dk), lambda b, h: (b, 0, q_col // dk + h)),
            pl.BlockSpec((1, seq, dk), lambda b, h: (b, 0, k_col // dk + h)),
            pl.BlockSpec((1, seq, dv), lambda b, h: (b, 0, v_col // dv + h)),
            pl.BlockSpec((1, seq, dv), lambda b, h: (b, 0, z_col // dv + h)),
            pl.BlockSpec((1, seq, LANES), lambda b, h: (b, 0, 0)),
            pl.BlockSpec((LANES, dk), lambda b, h: (0, h)),
            pl.BlockSpec((1, dk), lambda b, h: (0, h)),
            pl.BlockSpec((1, dv), lambda b, h: (0, h)),
        ],
        out_specs=pl.BlockSpec((1, seq, dv), lambda b, h: (b, 0, h)),
        out_shape=jax.ShapeDtypeStruct((bsz, seq, GLA_HEADS * dv), BF16),
        scratch_shapes=[pltpu.VMEM((seq, dk), F32)],
        compiler_params=pltpu.CompilerParams(
            dimension_semantics=("parallel", "parallel"),
            vmem_limit_bytes=VMEM_LIMIT_BYTES),
        name="gla",
    )(proj3, proj3, proj3, proj3, code3, w_alpha, b_alpha, gain)


def _out_kernel(a1_ref, a2_ref, a3_ref, s1_ref, s2_ref, s3_ref, za_ref, ob_ref, ga_ref, gb_ref, x_ref,
                expand_ref, wa_ref, wb_ref, wo_ref, bgate_ref, fgain_ref, out_ref, *, chunks):
    tm, d = x_ref.shape
    rc = tm // chunks
    expand = expand_ref[...]
    bg = bgate_ref[...]
    head_lane = lax.broadcasted_iota(jnp.int32, (rc, LANES), 1) < ATTN_HEADS
    ya, yb = [], []
    for c in range(chunks):
        rows = slice(c * rc, (c + 1) * rc)
        yb.append(_dot(ob_ref[rows, :], wb_ref[...]))
        stats = [s_ref[rows, :] for s_ref in (s1_ref, s2_ref, s3_ref)]
        m = jnp.maximum(jnp.maximum(stats[0], stats[1]), stats[2])
        scale = [jnp.exp2(s - m) for s in stats]
        den = [pltpu.roll(s, LANES - DEN_LANE, 1) for s in stats]
        inv = 1.0 / (scale[0] * den[0] + scale[1] * den[1] + scale[2] * den[2])
        oa = None
        for p, a_ref in enumerate((a1_ref, a2_ref, a3_ref)):
            weight = jnp.where(head_lane, scale[p] * inv, 0.0).astype(BF16)
            term = _dot(weight, expand) * a_ref[rows, :].astype(F32)
            oa = term if oa is None else oa + term
        za = za_ref[rows, :].astype(F32)
        ya.append(_dot((oa * (za * _sigmoid(za))).astype(BF16), wa_ref[...]))
    h = []
    for c in range(chunks):
        rows = slice(c * rc, (c + 1) * rc)
        gate_a = _sigmoid(ga_ref[rows, :].astype(F32) + bg[:, :d])
        gate_b = _sigmoid(gb_ref[rows, :].astype(F32) + bg[:, d:])
        merged = (gate_a * ya[c] + gate_b * yb[c]).astype(BF16)
        h.append(x_ref[rows, :] + _dot(merged, wo_ref[...]))
    for c in range(chunks):
        rows = slice(c * rc, (c + 1) * rc)
        ms = jnp.mean(h[c] * h[c], axis=-1, keepdims=True)
        out_ref[rows, :] = h[c] * lax.rsqrt(ms + NORM_EPS) * fgain_ref[...]


def _out_stage(a_pats, s_pats, proj2, ob2, x2, expand, wa, wb, wo, bgate, fgain, *,
               tm, chunks, za_col, ga_col, gb_col):
    t, d = x2.shape
    row_blk = lambda cb: pl.BlockSpec((tm, d), lambda i, cb=cb: (i, cb))
    stat_blk = pl.BlockSpec((tm, LANES), lambda i: (i, 0))
    full = lambda a: pl.BlockSpec(a.shape, lambda i: (0, 0))
    return pl.pallas_call(
        functools.partial(_out_kernel, chunks=chunks),
        grid=(t // tm,),
        in_specs=[row_blk(0), row_blk(0), row_blk(0), stat_blk, stat_blk, stat_blk,
                  row_blk(za_col // d), row_blk(0), row_blk(ga_col // d), row_blk(gb_col // d), row_blk(0),
                  full(expand), full(wa), full(wb), full(wo), full(bgate), full(fgain)],
        out_specs=row_blk(0),
        out_shape=jax.ShapeDtypeStruct((t, d), F32),
        compiler_params=pltpu.CompilerParams(
            dimension_semantics=("parallel",),
            vmem_limit_bytes=VMEM_LIMIT_BYTES),
        name="out_stage",
    )(*a_pats, *s_pats, proj2, ob2, proj2, proj2, x2, expand, wa, wb, wo, bgate, fgain)


def kernel(x, norm_gain, w_in, b_gate, w_alpha, b_alpha, gla_norm_gain,
           w_out_attn, w_out_gla, w_out, final_norm_gain):
    bsz, seq, d = x.shape
    assert w_in.shape[0] == 1, "single-layer block"
    aw = ATTN_HEADS * ATTN_HEAD_DIM
    gk = w_alpha.shape[-1]
    gv = gla_norm_gain.shape[-1]
    sizes = (aw, aw, aw, aw, gk, gk, gv, gv, GLA_GATE_RANK, d, d)
    offs = [0]
    for s in sizes:
        offs.append(offs[-1] + s)
    assert offs[-1] == w_in.shape[-1]
    code_lo, code_hi = offs[8], offs[9]
    n_qkv = offs[3]

    w = w_in[0]
    w_main = jnp.concatenate([w[:, :code_lo], w[:, code_hi:]], axis=1).astype(BF16)
    w_code = jnp.pad(w[:, code_lo:code_hi], ((0, 0), (0, LANES - GLA_GATE_RANK))).astype(BF16)
    w_alpha_p = jnp.pad(w_alpha[0], ((0, LANES - GLA_GATE_RANK), (0, 0)))
    slopes = 2.0 ** (-8.0 * (jnp.arange(ATTN_HEADS, dtype=F32) + 1.0) / ATTN_HEADS)
    head_of_col = jnp.arange(aw, dtype=jnp.int32) // ATTN_HEAD_DIM
    expand = (jnp.arange(LANES, dtype=jnp.int32)[:, None] == head_of_col[None, :]).astype(BF16)

    x2 = x.reshape(bsz * seq, d)
    qkv, code = _in_proj(x2, norm_gain, w_main[:, :n_qkv], w_code, out_dtype=F32, tm=2048, tn=768)
    rest, = _in_proj(x2, norm_gain, w_main[:, n_qkv:], None, out_dtype=BF16, tm=2048, tn=1024)
    qkv3 = qkv.reshape(bsz, seq, n_qkv)
    rest3 = rest.reshape(bsz, seq, -1)
    rel = lambda o: o - n_qkv - (GLA_GATE_RANK if o > code_lo else 0)

    pats = [_attention_pattern(qkv3, slopes, dilation=dil) for dil in DILATIONS]
    a_pats = [p[0] for p in pats]
    s_pats = [p[1] for p in pats]

    ob = _gla(rest3, code.reshape(bsz, seq, LANES), w_alpha_p, b_alpha, gla_norm_gain,
              q_col=rel(offs[4]), k_col=rel(offs[5]), v_col=rel(offs[6]), z_col=rel(offs[7]), group=8)

    out = _out_stage(a_pats, s_pats, rest, ob.reshape(bsz * seq, gv), x2, expand,
                     w_out_attn[0].astype(BF16), w_out_gla[0].astype(BF16), w_out[0].astype(BF16),
                     b_gate, final_norm_gain.reshape(1, d),
                     tm=512, chunks=2, za_col=rel(offs[3]), ga_col=rel(offs[9]), gb_col=rel(offs[10]))
    return out.reshape(bsz, seq, d)
```

```python
import functools
import math

import jax
import jax.numpy as jnp
from jax import lax
from jax.experimental import pallas as pl
from jax.experimental.pallas import tpu as pltpu

ATTN_HEADS = 16
ATTN_HEAD_DIM = 64
ATTN_WINDOW_STEPS = 128
DILATIONS = (1, 4, 16)
GLA_HEADS = 4
GLA_GATE_RANK = 16
GLA_GATE_TAU = 16.0
GLA_CHUNK = 64
NORM_EPS = 1e-6
GROUP_NORM_EPS = 1e-5

LANES = 128
VMEM_LIMIT_BYTES = 56 * 1024 * 1024

MASK_VALUE = -1e30
LOG2E = math.log2(math.e)
DEN_LANE = ATTN_HEADS
MOVE_ROWS = 512

F32 = jnp.float32
BF16 = jnp.bfloat16


def _dot(a, b):
    return jnp.dot(a, b, preferred_element_type=F32)


def _dot_nt(a, b):
    return lax.dot_general(a, b, (((1,), (1,)), ((), ())), preferred_element_type=F32)


def _dot_tn(a, b):
    return lax.dot_general(a, b, (((0,), (0,)), ((), ())), preferred_element_type=F32)


def _sigmoid(x):
    return 1.0 / (1.0 + jnp.exp(-x))


def _in_proj_kernel(x_ref, gain_ref, w_ref, wcode_ref, out_ref, code_ref, u_ref, *, first_tile_scale):
    j = pl.program_id(1)

    @pl.when(j == 0)
    def _():
        x = x_ref[...]
        ms = jnp.mean(x * x, axis=-1, keepdims=True)
        u = (x * lax.rsqrt(ms + NORM_EPS) * gain_ref[...]).astype(BF16)
        u_ref[...] = u
        code_ref[...] = _dot(u, wcode_ref[...])

    scale = jnp.where(j == 0, first_tile_scale, 1.0).astype(F32)
    out_ref[...] = (_dot(u_ref[...], w_ref[...]) * scale).astype(out_ref.dtype)


def _in_proj(x2, gain, w, w_code, *, first_tile_scale, tm, tn):
    t, d = x2.shape
    n = w.shape[1]
    return pl.pallas_call(
        functools.partial(_in_proj_kernel, first_tile_scale=first_tile_scale),
        grid=(t // tm, n // tn),
        in_specs=[
            pl.BlockSpec((tm, d), lambda i, j: (i, 0)),
            pl.BlockSpec((1, d), lambda i, j: (0, 0)),
            pl.BlockSpec((d, tn), lambda i, j: (0, j)),
            pl.BlockSpec((d, LANES), lambda i, j: (0, 0)),
        ],
        out_specs=[
            pl.BlockSpec((tm, tn), lambda i, j: (i, j)),
            pl.BlockSpec((tm, LANES), lambda i, j: (i, 0)),
        ],
        out_shape=[
            jax.ShapeDtypeStruct((t, n), BF16),
            jax.ShapeDtypeStruct((t, LANES), F32),
        ],
        scratch_shapes=[pltpu.VMEM((tm, d), BF16)],
        compiler_params=pltpu.CompilerParams(
            dimension_semantics=("parallel", "arbitrary"),
            vmem_limit_bytes=VMEM_LIMIT_BYTES),
        name="in_proj",
    )(x2, gain, w, w_code)


def _gather_items(src_ref, put, stage_ref, tmp_ref, dilation):
    assert dilation in (4, 16)
    seq = src_ref.shape[1]
    length = seq // dilation
    quarter = seq // 4
    pieces = quarter // MOVE_ROWS
    items = []

    def widen(rows):
        stage_ref[rows, :] = src_ref[0, rows, :].astype(F32)

    for c in range(seq // MOVE_ROWS):
        items.append(functools.partial(widen, slice(c * MOVE_ROWS, (c + 1) * MOVE_ROWS)))
    if dilation == 4:
        for r in range(4):
            for c in range(pieces):
                rows = slice(r * quarter + c * MOVE_ROWS, r * quarter + (c + 1) * MOVE_ROWS)
                src = pl.ds(r + 4 * c * MOVE_ROWS, MOVE_ROWS, stride=4)
                items.append(lambda rows=rows, src=src: put(rows, stage_ref[src, :]))
        return items

    def to_tmp(rows, src):
        tmp_ref[rows, :] = stage_ref[src, :]

    for r4 in range(4):
        for c in range(pieces):
            rows = slice(r4 * quarter + c * MOVE_ROWS, r4 * quarter + (c + 1) * MOVE_ROWS)
            src = pl.ds(r4 + 4 * c * MOVE_ROWS, MOVE_ROWS, stride=4)
            items.append(functools.partial(to_tmp, rows, src))
    for r4 in range(4):
        for q4 in range(4):
            r16 = r4 + 4 * q4
            rows = slice(r16 * length, (r16 + 1) * length)
            src = pl.ds(r4 * quarter + q4, length, stride=4)
            items.append(lambda rows=rows, src=src: put(rows, tmp_ref[src, :]))
    return items


def _scatter_items(src_ref, tmp_ref, sink, dilation):
    seq = src_ref.shape[0]
    length = seq // dilation
    quarter = seq // 4
    pieces = quarter // MOVE_ROWS
    items = []

    def move(dst_ref, dst, from_ref, rows):
        dst_ref[dst, :] = from_ref[rows, :]

    final_ref = src_ref
    if dilation == 4:
        for r in range(4):
            for c in range(pieces):
                rows = slice(r * quarter + c * MOVE_ROWS, r * quarter + (c + 1) * MOVE_ROWS)
                dst = pl.ds(r + 4 * c * MOVE_ROWS, MOVE_ROWS, stride=4)
                items.append(functools.partial(move, tmp_ref, dst, src_ref, rows))
        final_ref = tmp_ref
    elif dilation == 16:
        for r4 in range(4):
            for q4 in range(4):
                r16 = r4 + 4 * q4
                rows = slice(r16 * length, (r16 + 1) * length)
                dst = pl.ds(r4 * quarter + q4, length, stride=4)
                items.append(functools.partial(move, tmp_ref, dst, src_ref, rows))
        for r4 in range(4):
            for c in range(pieces):
                rows = slice(r4 * quarter + c * MOVE_ROWS, r4 * quarter + (c + 1) * MOVE_ROWS)
                dst = pl.ds(r4 + 4 * c * MOVE_ROWS, MOVE_ROWS, stride=4)
                items.append(functools.partial(move, src_ref, dst, tmp_ref, rows))
    else:
        assert dilation == 1
    for c in range(seq // MOVE_ROWS):
        rows = slice(c * MOVE_ROWS, (c + 1) * MOVE_ROWS)
        items.append(lambda rows=rows: sink(rows, final_ref[rows, :]))
    return items


def _attn_kernel(slopes_ref, qa_ref, ka_ref, va_ref, qb_ref, kb_ref, vb_ref, o_ref, stats_ref,
                 qs_ref, ks_ref, vs_ref, stage_ref, gtmp_ref, stmp_ref, acc0_ref, acc1_ref, st_ref, *, dilation):
    n = ATTN_WINDOW_STEPS
    seq = qa_ref.shape[1]
    n_blocks = seq // n
    blocks_per_class = seq // dilation // n
    step = pl.program_id(1)
    srcs = ((qa_ref, ka_ref, va_ref), (qb_ref, kb_ref, vb_ref))
    acc_refs = (acc0_ref, acc1_ref)

    @pl.when(step == 0)
    def _():
        st_ref[...] = jnp.zeros_like(st_ref)

    def put_q(group):
        def put(rows, vals):
            q = vals.astype(BF16)
            low = lax.broadcasted_iota(jnp.int32, q.shape, 1) < ATTN_HEAD_DIM
            zero = jnp.zeros_like(q)
            qs_ref[group, 0, rows, :] = jnp.where(low, q, zero)
            qs_ref[group, 1, rows, :] = jnp.where(low, zero, q)
        return put

    def put_plain(dst_ref, group):
        def put(rows, vals):
            dst_ref[group, rows, :] = vals.astype(BF16)
        return put

    def gather(group):
        q_src, k_src, v_src = srcs[group]
        if dilation == 1:
            put = put_q(group)
            return [functools.partial(lambda rows: put(rows, q_src[0, rows, :]),
                                      slice(c * MOVE_ROWS, (c + 1) * MOVE_ROWS)) for c in range(seq // MOVE_ROWS)]
        return (_gather_items(q_src, put_q(group), stage_ref, gtmp_ref, dilation)
                + _gather_items(k_src, put_plain(ks_ref, group), stage_ref, gtmp_ref, dilation)
                + _gather_items(v_src, put_plain(vs_ref, group), stage_ref, gtmp_ref, dilation))

    def k_rows(group, rows):
        return srcs[group][1][0, rows, :] if dilation == 1 else ks_ref[group, rows, :]

    def v_rows(group, rows):
        return srcs[group][2][0, rows, :] if dilation == 1 else vs_ref[group, rows, :]

    def scatter(group):
        def sink(rows, vals):
            o_ref[0, rows, group * LANES:(group + 1) * LANES] = vals.astype(o_ref.dtype)
        return _scatter_items(acc_refs[group], stmp_ref, sink, dilation)

    row = lax.broadcasted_iota(jnp.int32, (2 * n, 2 * n), 0)
    col = lax.broadcasted_iota(jnp.int32, (2 * n, 2 * n), 1)
    steps = n + (row % n) - col
    band = (steps >= 0) & (steps <= n)
    dist = (steps * dilation).astype(F32) * LOG2E
    lane = lax.broadcasted_iota(jnp.int32, (n, LANES), 1)
    low_half = lane < ATTN_HEAD_DIM

    def run_group(group, side_items):
        pair = 2 * step + group
        slope = jnp.where(row < n, slopes_ref[2 * pair], slopes_ref[2 * pair + 1])
        bias = jnp.where(band, -slope * dist, MASK_VALUE)
        bias_first = jnp.where(col < n, MASK_VALUE, bias)
        is_m0, is_m1 = lane == 2 * pair, lane == 2 * pair + 1
        is_d0, is_d1 = lane == DEN_LANE + 2 * pair, lane == DEN_LANE + 2 * pair + 1
        acc_ref = acc_refs[group]

        def block_rows(g):
            return slice(g * n, (g + 1) * n), slice(max(g - 1, 0) * n, max(g, 1) * n)

        def scores(g):
            rows, prev = block_rows(g)
            qq = jnp.concatenate([qs_ref[group, 0, rows, :], qs_ref[group, 1, rows, :]], axis=0)
            kk = jnp.concatenate([k_rows(group, prev), k_rows(group, rows)], axis=0)
            first = g % blocks_per_class == 0
            return _dot_nt(qq, kk) + (bias_first if first else bias)

        def finish(g, s):
            rows, prev = block_rows(g)
            m = jnp.max(s, axis=-1, keepdims=True)
            e = jnp.exp2(s - m)
            den = jnp.sum(e, axis=-1, keepdims=True)
            vv = jnp.concatenate([v_rows(group, prev), v_rows(group, rows)], axis=0)
            pv = _dot(e.astype(BF16), vv)
            acc_ref[rows, :] = jnp.where(low_half, pv[:n], pv[n:])
            cur = st_ref[rows, :]
            cur = jnp.where(is_m0, m[:n], jnp.where(is_m1, m[n:], cur))
            cur = jnp.where(is_d0, den[:n], jnp.where(is_d1, den[n:], cur))
            st_ref[rows, :] = cur

        s_next = scores(0)
        for g in range(n_blocks):
            s_cur = s_next
            if g + 1 < n_blocks:
                s_next = scores(g + 1)
            finish(g, s_cur)
            for item in side_items[g * len(side_items) // n_blocks:(g + 1) * len(side_items) // n_blocks]:
                item()

    for item in gather(0):
        item()
    run_group(0, gather(1))
    run_group(1, scatter(0))
    for item in scatter(1):
        item()

    @pl.when(step == pl.num_programs(1) - 1)
    def _():
        def sink(rows, vals):
            stats_ref[0, rows, :] = vals
        for item in _scatter_items(st_ref, stmp_ref, sink, dilation):
            item()


def _attention_pattern(proj3, slopes, *, dilation):
    bsz, seq, _ = proj3.shape
    width = ATTN_HEADS * ATTN_HEAD_DIM
    assert seq % (ATTN_WINDOW_STEPS * dilation) == 0 and seq % (4 * MOVE_ROWS) == 0
    kb = width // LANES

    def in_map(section, group):
        return lambda b, s, *_: (b, 0, section * kb + 2 * s + group)

    kernel = functools.partial(_attn_kernel, dilation=dilation)
    tile = lambda dtype: pltpu.VMEM((seq, LANES), dtype)
    o, stats = pl.pallas_call(
        kernel,
        grid_spec=pltpu.PrefetchScalarGridSpec(
            num_scalar_prefetch=1,
            grid=(bsz, kb // 2),
            in_specs=[pl.BlockSpec((1, seq, LANES), in_map(section, group))
                      for group in range(2) for section in range(3)],
            out_specs=[
                pl.BlockSpec((1, seq, 2 * LANES), lambda b, s, *_: (b, 0, s)),
                pl.BlockSpec((1, seq, LANES), lambda b, s, *_: (b, 0, 0)),
            ],
            scratch_shapes=[
                pltpu.VMEM((2, 2, seq, LANES), BF16),
                pltpu.VMEM((2, seq, LANES), BF16),
                pltpu.VMEM((2, seq, LANES), BF16),
                tile(F32),
                tile(F32),
                tile(F32),
                tile(F32),
                tile(F32),
                tile(F32),
            ],
        ),
        out_shape=[
            jax.ShapeDtypeStruct((bsz, seq, width), BF16),
            jax.ShapeDtypeStruct((bsz, seq, LANES), F32),
        ],
        compiler_params=pltpu.CompilerParams(
            dimension_semantics=("parallel", "arbitrary"),
            vmem_limit_bytes=VMEM_LIMIT_BYTES),
        name=f"dilated_attn_d{dilation}",
    )(slopes, *([proj3] * 6))
    return o.reshape(bsz * seq, width), stats.reshape(bsz * seq, LANES)


def _split_bf16(x):
    hi = x.astype(BF16)
    return hi, (x - hi.astype(F32)).astype(BF16)


def _gla_kernel(q_ref, k_ref, v_ref, z_ref, code_ref, walpha_ref, balpha_ref, gain_ref,
                o_ref, b_ref, *, group):
    c = GLA_CHUNK
    seq, dk = q_ref.shape[1], q_ref.shape[2]
    dv = v_ref.shape[2]
    n_groups = seq // (c * group)

    ri = lax.broadcasted_iota(jnp.int32, (c, c), 0)
    ci = lax.broadcasted_iota(jnp.int32, (c, c), 1)
    causal = ri >= ci
    tri = causal.astype(BF16)
    gain = gain_ref[...]
    w_hi, w_lo = _split_bf16(walpha_ref[...])
    b_alpha = balpha_ref[...]

    def chunk_rows(g, u):
        return slice((g * group + u) * c, (g * group + u + 1) * c)

    def gate_logs(g):
        rows = slice(g * group * c, (g + 1) * group * c)
        code_hi, code_lo = _split_bf16(code_ref[0, rows, :])
        logits = _dot(code_hi, w_hi) + _dot(code_hi, w_lo) + _dot(code_lo, w_hi) + b_alpha
        log_sig = jnp.minimum(logits, 0.0) - jnp.log(1.0 + jnp.exp(-jnp.abs(logits)))
        b_ref[rows, :] = log_sig / GLA_GATE_TAU

    def cumsums(g):
        for u in range(group):
            rows = chunk_rows(g, u)
            la = b_ref[rows, :]
            la1 = la.astype(BF16)
            rem = la - la1.astype(F32)
            la2 = rem.astype(BF16)
            la3 = (rem - la2.astype(F32)).astype(BF16)
            b_ref[rows, :] = _dot(tri, la1) + _dot(tri, la2) + _dot(tri, la3)

    gate_logs(0)
    cumsums(0)
    st = jnp.zeros((dv, dk), F32)
    for g in range(n_groups):
        rows = [chunk_rows(g, u) for u in range(group)]
        q_dec, k_inv, k_end, decay, v = [], [], [], [], []
        for u in range(group):
            b = b_ref[rows[u], :]
            b_last = b[c - 1:c, :]
            q = q_ref[0, rows[u], :].astype(F32) * (dk ** -0.5)
            k = k_ref[0, rows[u], :].astype(F32)
            q_dec.append((q * jnp.exp(b)).astype(BF16))
            k_inv.append((k * jnp.exp(-b)).astype(BF16))
            k_end.append((k * jnp.exp(b_last - b)).astype(BF16))
            decay.append(jnp.exp(b_last))
            v.append(v_ref[0, rows[u], :])
        a = [jnp.where(causal, _dot_nt(q_dec[u], k_inv[u]), 0.0).astype(BF16) for u in range(group)]
        d_state = [_dot_tn(v[u], k_end[u]) for u in range(group)]
        o_intra = [_dot(a[u], v[u]) for u in range(group)]
        if g + 1 < n_groups:
            gate_logs(g + 1)
        o = []
        for u in range(group):
            o.append(o_intra[u] + _dot_nt(q_dec[u], st.astype(BF16)))
            st = decay[u] * st + d_state[u]
        if g + 1 < n_groups:
            cumsums(g + 1)
        for u in range(group):
            mu = jnp.mean(o[u], axis=-1, keepdims=True)
            cen = o[u] - mu
            var = jnp.mean(cen * cen, axis=-1, keepdims=True)
            y = cen * lax.rsqrt(var + GROUP_NORM_EPS) * gain
            z = z_ref[0, rows[u], :].astype(F32)
            o_ref[0, rows[u], :] = (y * (z * _sigmoid(z))).astype(o_ref.dtype)


def _gla(proj3, code3, w_alpha, b_alpha, gain, *, q_col, k_col, v_col, z_col, group):
    bsz, seq, _ = proj3.shape
    dk = w_alpha.shape[1] // GLA_HEADS
    dv = gain.shape[1] // GLA_HEADS
    assert seq % (GLA_CHUNK * group) == 0
    kernel = functools.partial(_gla_kernel, group=group)
    return pl.pallas_call(
        kernel,
        grid=(bsz, GLA_HEADS),
        in_specs=[
            pl.BlockSpec((1, seq, dk), lambda b, h: (b, 0, q_col // dk + h)),
            pl.BlockSpec((1, seq, dk), lambda b, h: (b, 0, k_col // dk + h)),
            pl.BlockSpec((1, seq, dv), lambda b, h: (b, 0, v_col // dv + h)),
            pl.BlockSpec((1, seq, dv), lambda b, h: (b, 0, z_col // dv + h)),
            pl.BlockSpec((1, seq, LANES), lambda b, h: (b, 0, 0)),
            pl.BlockSpec((LANES, dk), lambda b, h: (0, h)),
            pl.BlockSpec((1, dk), lambda b, h: (0, h)),
            pl.BlockSpec((1, dv), lambda b, h: (0, h)),
        ],
        out_specs=pl.BlockSpec((1, seq, dv), lambda b, h: (b, 0, h)),
        out_shape=jax.ShapeDtypeStruct((bsz, seq, GLA_HEADS * dv), BF16),
        scratch_shapes=[pltpu.VMEM((seq, dk), F32)],
        compiler_params=pltpu.CompilerParams(
            dimension_semantics=("parallel", "parallel"),
            vmem_limit_bytes=VMEM_LIMIT_BYTES),
        name="gla",
    )(proj3, proj3, proj3, proj3, code3, w_alpha, b_alpha, gain)


def _out_kernel(a1_ref, a2_ref, a3_ref, s1_ref, s2_ref, s3_ref, za_ref, ob_ref, ga_ref, gb_ref, x_ref,
                expand_ref, wa_ref, wb_ref, wo_ref, bgate_ref, fgain_ref, out_ref, *, chunks):
    tm, d = x_ref.shape
    rc = tm // chunks
    expand = expand_ref[...]
    bg = bgate_ref[...]
    head_lane = lax.broadcasted_iota(jnp.int32, (rc, LANES), 1) < ATTN_HEADS
    ya, yb = [], []
    for c in range(chunks):
        rows = slice(c * rc, (c + 1) * rc)
        yb.append(_dot(ob_ref[rows, :], wb_ref[...]))
        stats = [s_ref[rows, :] for s_ref in (s1_ref, s2_ref, s3_ref)]
        m = jnp.maximum(jnp.maximum(stats[0], stats[1]), stats[2])
        scale = [jnp.exp2(s - m) for s in stats]
        den = [pltpu.roll(s, LANES - DEN_LANE, 1) for s in stats]
        inv = 1.0 / (scale[0] * den[0] + scale[1] * den[1] + scale[2] * den[2])
        oa = None
        for p, a_ref in enumerate((a1_ref, a2_ref, a3_ref)):
            weight = jnp.where(head_lane, scale[p] * inv, 0.0).astype(BF16)
            term = _dot(weight, expand) * a_ref[rows, :].astype(F32)
            oa = term if oa is None else oa + term
        za = za_ref[rows, :].astype(F32)
        ya.append(_dot((oa * (za * _sigmoid(za))).astype(BF16), wa_ref[...]))
    h = []
    for c in range(chunks):
        rows = slice(c * rc, (c + 1) * rc)
        gate_a = _sigmoid(ga_ref[rows, :].astype(F32) + bg[:, :d])
        gate_b = _sigmoid(gb_ref[rows, :].astype(F32) + bg[:, d:])
        merged = (gate_a * ya[c] + gate_b * yb[c]).astype(BF16)
        h.append(x_ref[rows, :] + _dot(merged, wo_ref[...]))
    for c in range(chunks):
        rows = slice(c * rc, (c + 1) * rc)
        ms = jnp.mean(h[c] * h[c], axis=-1, keepdims=True)
        out_ref[rows, :] = h[c] * lax.rsqrt(ms + NORM_EPS) * fgain_ref[...]


def _out_stage(a_pats, s_pats, proj2, ob2, x2, expand, wa, wb, wo, bgate, fgain, *,
               tm, chunks, za_col, ga_col, gb_col):
    t, d = x2.shape
    row_blk = lambda cb: pl.BlockSpec((tm, d), lambda i, cb=cb: (i, cb))
    stat_blk = pl.BlockSpec((tm, LANES), lambda i: (i, 0))
    full = lambda a: pl.BlockSpec(a.shape, lambda i: (0, 0))
    return pl.pallas_call(
        functools.partial(_out_kernel, chunks=chunks),
        grid=(t // tm,),
        in_specs=[row_blk(0), row_blk(0), row_blk(0), stat_blk, stat_blk, stat_blk,
                  row_blk(za_col // d), row_blk(0), row_blk(ga_col // d), row_blk(gb_col // d), row_blk(0),
                  full(expand), full(wa), full(wb), full(wo), full(bgate), full(fgain)],
        out_specs=row_blk(0),
        out_shape=jax.ShapeDtypeStruct((t, d), F32),
        compiler_params=pltpu.CompilerParams(
            dimension_semantics=("parallel",),
            vmem_limit_bytes=VMEM_LIMIT_BYTES),
        name="out_stage",
    )(*a_pats, *s_pats, proj2, ob2, proj2, proj2, x2, expand, wa, wb, wo, bgate, fgain)


def kernel(x, norm_gain, w_in, b_gate, w_alpha, b_alpha, gla_norm_gain,
           w_out_attn, w_out_gla, w_out, final_norm_gain):
    bsz, seq, d = x.shape
    assert w_in.shape[0] == 1, "single-layer block"
    aw = ATTN_HEADS * ATTN_HEAD_DIM
    gk = w_alpha.shape[-1]
    gv = gla_norm_gain.shape[-1]
    sizes = (aw, aw, aw, aw, gk, gk, gv, gv, GLA_GATE_RANK, d, d)
    offs = [0]
    for s in sizes:
        offs.append(offs[-1] + s)
    assert offs[-1] == w_in.shape[-1]
    code_lo, code_hi = offs[8], offs[9]

    w = w_in[0]
    w_main = jnp.concatenate([w[:, :code_lo], w[:, code_hi:]], axis=1).astype(BF16)
    w_code = jnp.pad(w[:, code_lo:code_hi], ((0, 0), (0, LANES - GLA_GATE_RANK))).astype(BF16)
    w_alpha_p = jnp.pad(w_alpha[0], ((0, LANES - GLA_GATE_RANK), (0, 0)))
    slopes = 2.0 ** (-8.0 * (jnp.arange(ATTN_HEADS, dtype=F32) + 1.0) / ATTN_HEADS)
    head_of_col = jnp.arange(aw, dtype=jnp.int32) // ATTN_HEAD_DIM
    expand = (jnp.arange(LANES, dtype=jnp.int32)[:, None] == head_of_col[None, :]).astype(BF16)

    x2 = x.reshape(bsz * seq, d)
    proj, code = _in_proj(x2, norm_gain, w_main, w_code, first_tile_scale=ATTN_HEAD_DIM ** -0.5 * LOG2E,
                          tm=2048, tn=aw)
    proj3 = proj.reshape(bsz, seq, -1)
    rel = lambda o: o - (GLA_GATE_RANK if o > code_lo else 0)

    pats = [_attention_pattern(proj3, slopes, dilation=dil) for dil in DILATIONS]
    a_pats = [p[0] for p in pats]
    s_pats = [p[1] for p in pats]

    ob = _gla(proj3, code.reshape(bsz, seq, LANES), w_alpha_p, b_alpha, gla_norm_gain,
              q_col=rel(offs[4]), k_col=rel(offs[5]), v_col=rel(offs[6]), z_col=rel(offs[7]), group=8)

    out = _out_stage(a_pats, s_pats, proj, ob.reshape(bsz * seq, gv), x2, expand,
                     w_out_attn[0].astype(BF16), w_out_gla[0].astype(BF16), w_out[0].astype(BF16),
                     b_gate, final_norm_gain.reshape(1, d),
                     tm=512, chunks=2, za_col=rel(offs[3]), ga_col=rel(offs[9]), gb_col=rel(offs[10]))
    return out.reshape(bsz, seq, d)
```

```python
import functools
import math

import jax
import jax.numpy as jnp
from jax import lax
from jax.experimental import pallas as pl
from jax.experimental.pallas import tpu as pltpu

ATTN_HEADS = 16
ATTN_HEAD_DIM = 64
ATTN_WINDOW_STEPS = 128
DILATIONS = (1, 4, 16)
GLA_HEADS = 4
GLA_GATE_RANK = 16
GLA_GATE_TAU = 16.0
GLA_CHUNK = 64
NORM_EPS = 1e-6
GROUP_NORM_EPS = 1e-5

LANES = 128
VMEM_LIMIT_BYTES = 56 * 1024 * 1024

MASK_VALUE = -1e30
LOG2E = math.log2(math.e)
DEN_LANE = ATTN_HEADS
MOVE_ROWS = 512

F32 = jnp.float32
BF16 = jnp.bfloat16


def _dot(a, b):
    return jnp.dot(a, b, preferred_element_type=F32)


def _dot_nt(a, b):
    return lax.dot_general(a, b, (((1,), (1,)), ((), ())), preferred_element_type=F32)


def _dot_tn(a, b):
    return lax.dot_general(a, b, (((0,), (0,)), ((), ())), preferred_element_type=F32)


def _sigmoid(x):
    return 1.0 / (1.0 + jnp.exp(-x))


def _in_proj_kernel(x_ref, gain_ref, wf_ref, wg_ref, wcode_ref, out_ref, code_ref, u_ref, *,
                    n_front, first_tile_scale):
    j = pl.program_id(1)

    @pl.when(j == 0)
    def _():
        x = x_ref[...]
        ms = jnp.mean(x * x, axis=-1, keepdims=True)
        u = (x * lax.rsqrt(ms + NORM_EPS) * gain_ref[...]).astype(BF16)
        u_ref[...] = u
        code_ref[...] = _dot(u, wcode_ref[...])

    @pl.when(j < n_front)
    def _():
        scale = jnp.where(j == 0, first_tile_scale, 1.0).astype(F32)
        out_ref[...] = (_dot(u_ref[...], wf_ref[0].astype(BF16)) * scale).astype(out_ref.dtype)

    @pl.when(j >= n_front)
    def _():
        out_ref[...] = _dot(u_ref[...], wg_ref[...]).astype(out_ref.dtype)


def _in_proj(x2, gain, w_all, n_front_cols, w_tail, w_code, *, first_tile_scale, tm, tn):
    t, d = x2.shape
    n_front = n_front_cols // tn
    n = n_front_cols + w_tail.shape[1]
    assert n_front_cols % tn == 0 and w_tail.shape[1] % tn == 0
    return pl.pallas_call(
        functools.partial(_in_proj_kernel, n_front=n_front, first_tile_scale=first_tile_scale),
        grid=(t // tm, n // tn),
        in_specs=[
            pl.BlockSpec((tm, d), lambda i, j: (i, 0)),
            pl.BlockSpec((1, d), lambda i, j: (0, 0)),
            pl.BlockSpec((1, d, tn), lambda i, j: (0, 0, jnp.minimum(j, n_front - 1))),
            pl.BlockSpec((d, tn), lambda i, j: (0, jnp.maximum(j - n_front, 0))),
            pl.BlockSpec((d, LANES), lambda i, j: (0, 0)),
        ],
        out_specs=[
            pl.BlockSpec((tm, tn), lambda i, j: (i, j)),
            pl.BlockSpec((tm, LANES), lambda i, j: (i, 0)),
        ],
        out_shape=[
            jax.ShapeDtypeStruct((t, n), BF16),
            jax.ShapeDtypeStruct((t, LANES), F32),
        ],
        scratch_shapes=[pltpu.VMEM((tm, d), BF16)],
        compiler_params=pltpu.CompilerParams(
            dimension_semantics=("parallel", "arbitrary"),
            vmem_limit_bytes=VMEM_LIMIT_BYTES),
        name="in_proj",
    )(x2, gain, w_all, w_tail, w_code)


def _gather_items(src_ref, put, stage_ref, tmp_ref, dilation):
    assert dilation in (4, 16)
    seq = src_ref.shape[1]
    length = seq // dilation
    quarter = seq // 4
    pieces = quarter // MOVE_ROWS
    items = []

    def widen(rows):
        stage_ref[rows, :] = src_ref[0, rows, :].astype(F32)

    for c in range(seq // MOVE_ROWS):
        items.append(functools.partial(widen, slice(c * MOVE_ROWS, (c + 1) * MOVE_ROWS)))
    if dilation == 4:
        for r in range(4):
            for c in range(pieces):
                rows = slice(r * quarter + c * MOVE_ROWS, r * quarter + (c + 1) * MOVE_ROWS)
                src = pl.ds(r + 4 * c * MOVE_ROWS, MOVE_ROWS, stride=4)
                items.append(lambda rows=rows, src=src: put(rows, stage_ref[src, :]))
        return items

    def to_tmp(rows, src):
        tmp_ref[rows, :] = stage_ref[src, :]

    for r4 in range(4):
        for c in range(pieces):
            rows = slice(r4 * quarter + c * MOVE_ROWS, r4 * quarter + (c + 1) * MOVE_ROWS)
            src = pl.ds(r4 + 4 * c * MOVE_ROWS, MOVE_ROWS, stride=4)
            items.append(functools.partial(to_tmp, rows, src))
    for r4 in range(4):
        for q4 in range(4):
            r16 = r4 + 4 * q4
            rows = slice(r16 * length, (r16 + 1) * length)
            src = pl.ds(r4 * quarter + q4, length, stride=4)
            items.append(lambda rows=rows, src=src: put(rows, tmp_ref[src, :]))
    return items


def _scatter_items(src_ref, tmp_ref, sink, dilation):
    seq = src_ref.shape[0]
    length = seq // dilation
    quarter = seq // 4
    pieces = quarter // MOVE_ROWS
    items = []

    def move(dst_ref, dst, from_ref, rows):
        dst_ref[dst, :] = from_ref[rows, :]

    final_ref = src_ref
    if dilation == 4:
        for r in range(4):
            for c in range(pieces):
                rows = slice(r * quarter + c * MOVE_ROWS, r * quarter + (c + 1) * MOVE_ROWS)
                dst = pl.ds(r + 4 * c * MOVE_ROWS, MOVE_ROWS, stride=4)
                items.append(functools.partial(move, tmp_ref, dst, src_ref, rows))
        final_ref = tmp_ref
    elif dilation == 16:
        for r4 in range(4):
            for q4 in range(4):
                r16 = r4 + 4 * q4
                rows = slice(r16 * length, (r16 + 1) * length)
                dst = pl.ds(r4 * quarter + q4, length, stride=4)
                items.append(functools.partial(move, tmp_ref, dst, src_ref, rows))
        for r4 in range(4):
            for c in range(pieces):
                rows = slice(r4 * quarter + c * MOVE_ROWS, r4 * quarter + (c + 1) * MOVE_ROWS)
                dst = pl.ds(r4 + 4 * c * MOVE_ROWS, MOVE_ROWS, stride=4)
                items.append(functools.partial(move, src_ref, dst, tmp_ref, rows))
    else:
        assert dilation == 1
    for c in range(seq // MOVE_ROWS):
        rows = slice(c * MOVE_ROWS, (c + 1) * MOVE_ROWS)
        items.append(lambda rows=rows: sink(rows, final_ref[rows, :]))
    return items


def _attn_kernel(slopes_ref, qa_ref, ka_ref, va_ref, qb_ref, kb_ref, vb_ref, o_ref, stats_ref,
                 qs_ref, ks_ref, vs_ref, stage_ref, gtmp_ref, stmp_ref, acc0_ref, acc1_ref, st_ref, *, dilation):
    n = ATTN_WINDOW_STEPS
    seq = qa_ref.shape[1]
    n_blocks = seq // n
    blocks_per_class = seq // dilation // n
    step = pl.program_id(1)
    srcs = ((qa_ref, ka_ref, va_ref), (qb_ref, kb_ref, vb_ref))
    acc_refs = (acc0_ref, acc1_ref)

    @pl.when(step == 0)
    def _():
        st_ref[...] = jnp.zeros_like(st_ref)

    def put_q(group):
        def put(rows, vals):
            q = vals.astype(BF16)
            low = lax.broadcasted_iota(jnp.int32, q.shape, 1) < ATTN_HEAD_DIM
            zero = jnp.zeros_like(q)
            qs_ref[group, 0, rows, :] = jnp.where(low, q, zero)
            qs_ref[group, 1, rows, :] = jnp.where(low, zero, q)
        return put

    def put_plain(dst_ref, group):
        def put(rows, vals):
            dst_ref[group, rows, :] = vals.astype(BF16)
        return put

    def gather(group):
        q_src, k_src, v_src = srcs[group]
        if dilation == 1:
            put = put_q(group)
            return [functools.partial(lambda rows: put(rows, q_src[0, rows, :]),
                                      slice(c * MOVE_ROWS, (c + 1) * MOVE_ROWS)) for c in range(seq // MOVE_ROWS)]
        return (_gather_items(q_src, put_q(group), stage_ref, gtmp_ref, dilation)
                + _gather_items(k_src, put_plain(ks_ref, group), stage_ref, gtmp_ref, dilation)
                + _gather_items(v_src, put_plain(vs_ref, group), stage_ref, gtmp_ref, dilation))

    def k_rows(group, rows):
        return srcs[group][1][0, rows, :] if dilation == 1 else ks_ref[group, rows, :]

    def v_rows(group, rows):
        return srcs[group][2][0, rows, :] if dilation == 1 else vs_ref[group, rows, :]

    def scatter(group):
        def sink(rows, vals):
            o_ref[0, rows, group * LANES:(group + 1) * LANES] = vals.astype(o_ref.dtype)
        return _scatter_items(acc_refs[group], stmp_ref, sink, dilation)

    row = lax.broadcasted_iota(jnp.int32, (2 * n, 2 * n), 0)
    col = lax.broadcasted_iota(jnp.int32, (2 * n, 2 * n), 1)
    steps = n + (row % n) - col
    band = (steps >= 0) & (steps <= n)
    dist = (steps * dilation).astype(F32) * LOG2E
    lane = lax.broadcasted_iota(jnp.int32, (n, LANES), 1)
    low_half = lane < ATTN_HEAD_DIM

    def run_group(group, side_items):
        pair = 2 * step + group
        slope = jnp.where(row < n, slopes_ref[2 * pair], slopes_ref[2 * pair + 1])
        bias = jnp.where(band, -slope * dist, MASK_VALUE)
        bias_first = jnp.where(col < n, MASK_VALUE, bias)
        is_m0, is_m1 = lane == 2 * pair, lane == 2 * pair + 1
        is_d0, is_d1 = lane == DEN_LANE + 2 * pair, lane == DEN_LANE + 2 * pair + 1
        acc_ref = acc_refs[group]

        def block_rows(g):
            return slice(g * n, (g + 1) * n), slice(max(g - 1, 0) * n, max(g, 1) * n)

        def scores(g):
            rows, prev = block_rows(g)
            qq = jnp.concatenate([qs_ref[group, 0, rows, :], qs_ref[group, 1, rows, :]], axis=0)
            kk = jnp.concatenate([k_rows(group, prev), k_rows(group, rows)], axis=0)
            first = g % blocks_per_class == 0
            return _dot_nt(qq, kk) + (bias_first if first else bias)

        def finish(g, s):
            rows, prev = block_rows(g)
            m = jnp.max(s, axis=-1, keepdims=True)
            e = jnp.exp2(s - m)
            den = jnp.sum(e, axis=-1, keepdims=True)
            vv = jnp.concatenate([v_rows(group, prev), v_rows(group, rows)], axis=0)
            pv = _dot(e.astype(BF16), vv)
            acc_ref[rows, :] = jnp.where(low_half, pv[:n], pv[n:])
            cur = st_ref[rows, :]
            cur = jnp.where(is_m0, m[:n], jnp.where(is_m1, m[n:], cur))
            cur = jnp.where(is_d0, den[:n], jnp.where(is_d1, den[n:], cur))
            st_ref[rows, :] = cur

        s_next = scores(0)
        for g in range(n_blocks):
            s_cur = s_next
            if g + 1 < n_blocks:
                s_next = scores(g + 1)
            finish(g, s_cur)
            for item in side_items[g * len(side_items) // n_blocks:(g + 1) * len(side_items) // n_blocks]:
                item()

    for item in gather(0):
        item()
    run_group(0, gather(1))
    run_group(1, scatter(0))
    for item in scatter(1):
        item()

    @pl.when(step == pl.num_programs(1) - 1)
    def _():
        def sink(rows, vals):
            stats_ref[0, rows, :] = vals
        for item in _scatter_items(st_ref, stmp_ref, sink, dilation):
            item()


def _attention_pattern(proj3, slopes, *, dilation):
    bsz, seq, _ = proj3.shape
    width = ATTN_HEADS * ATTN_HEAD_DIM
    assert seq % (ATTN_WINDOW_STEPS * dilation) == 0 and seq % (4 * MOVE_ROWS) == 0
    kb = width // LANES

    def in_map(section, group):
        return lambda b, s, *_: (b, 0, section * kb + 2 * s + group)

    kernel = functools.partial(_attn_kernel, dilation=dilation)
    tile = lambda dtype: pltpu.VMEM((seq, LANES), dtype)
    o, stats = pl.pallas_call(
        kernel,
        grid_spec=pltpu.PrefetchScalarGridSpec(
            num_scalar_prefetch=1,
            grid=(bsz, kb // 2),
            in_specs=[pl.BlockSpec((1, seq, LANES), in_map(section, group))
                      for group in range(2) for section in range(3)],
            out_specs=[
                pl.BlockSpec((1, seq, 2 * LANES), lambda b, s, *_: (b, 0, s)),
                pl.BlockSpec((1, seq, LANES), lambda b, s, *_: (b, 0, 0)),
            ],
            scratch_shapes=[
                pltpu.VMEM((2, 2, seq, LANES), BF16),
                pltpu.VMEM((2, seq, LANES), BF16),
                pltpu.VMEM((2, seq, LANES), BF16),
                tile(F32),
                tile(F32),
                tile(F32),
                tile(F32),
                tile(F32),
                tile(F32),
            ],
        ),
        out_shape=[
            jax.ShapeDtypeStruct((bsz, seq, width), BF16),
            jax.ShapeDtypeStruct((bsz, seq, LANES), F32),
        ],
        compiler_params=pltpu.CompilerParams(
            dimension_semantics=("parallel", "arbitrary"),
            vmem_limit_bytes=VMEM_LIMIT_BYTES),
        name=f"dilated_attn_d{dilation}",
    )(slopes, *([proj3] * 6))
    return o.reshape(bsz * seq, width), stats.reshape(bsz * seq, LANES)


def _split_bf16(x):
    hi = x.astype(BF16)
    return hi, (x - hi.astype(F32)).astype(BF16)


def _gla_kernel(q_ref, k_ref, v_ref, z_ref, code_ref, walpha_ref, balpha_ref, gain_ref,
                o_ref, b_ref, *, group):
    c = GLA_CHUNK
    seq, dk = q_ref.shape[1], q_ref.shape[2]
    dv = v_ref.shape[2]
    n_groups = seq // (c * group)

    ri = lax.broadcasted_iota(jnp.int32, (c, c), 0)
    ci = lax.broadcasted_iota(jnp.int32, (c, c), 1)
    causal = ri >= ci
    tri = causal.astype(BF16)
    gain = gain_ref[...]
    w_hi, w_lo = _split_bf16(walpha_ref[...])
    b_alpha = balpha_ref[...]

    def chunk_rows(g, u):
        return slice((g * group + u) * c, (g * group + u + 1) * c)

    def gate_logs(g):
        rows = slice(g * group * c, (g + 1) * group * c)
        code_hi, code_lo = _split_bf16(code_ref[0, rows, :])
        logits = _dot(code_hi, w_hi) + _dot(code_hi, w_lo) + _dot(code_lo, w_hi) + b_alpha
        log_sig = jnp.minimum(logits, 0.0) - jnp.log(1.0 + jnp.exp(-jnp.abs(logits)))
        b_ref[rows, :] = log_sig / GLA_GATE_TAU

    def cumsums(g):
        for u in range(group):
            rows = chunk_rows(g, u)
            la = b_ref[rows, :]
            la1 = la.astype(BF16)
            rem = la - la1.astype(F32)
            la2 = rem.astype(BF16)
            la3 = (rem - la2.astype(F32)).astype(BF16)
            b_ref[rows, :] = _dot(tri, la1) + _dot(tri, la2) + _dot(tri, la3)

    gate_logs(0)
    cumsums(0)
    st = jnp.zeros((dv, dk), F32)
    for g in range(n_groups):
        rows = [chunk_rows(g, u) for u in range(group)]
        q_dec, k_inv, k_end, decay, v = [], [], [], [], []
        for u in range(group):
            b = b_ref[rows[u], :]
            b_last = b[c - 1:c, :]
            q = q_ref[0, rows[u], :].astype(F32) * (dk ** -0.5)
            k = k_ref[0, rows[u], :].astype(F32)
            q_dec.append((q * jnp.exp(b)).astype(BF16))
            k_inv.append((k * jnp.exp(-b)).astype(BF16))
            k_end.append((k * jnp.exp(b_last - b)).astype(BF16))
            decay.append(jnp.exp(b_last))
            v.append(v_ref[0, rows[u], :])
        a = [jnp.where(causal, _dot_nt(q_dec[u], k_inv[u]), 0.0).astype(BF16) for u in range(group)]
        d_state = [_dot_tn(v[u], k_end[u]) for u in range(group)]
        o_intra = [_dot(a[u], v[u]) for u in range(group)]
        if g + 1 < n_groups:
            gate_logs(g + 1)
        o = []
        for u in range(group):
            o.append(o_intra[u] + _dot_nt(q_dec[u], st.astype(BF16)))
            st = decay[u] * st + d_state[u]
        if g + 1 < n_groups:
            cumsums(g + 1)
        for u in range(group):
            mu = jnp.mean(o[u], axis=-1, keepdims=True)
            cen = o[u] - mu
            var = jnp.mean(cen * cen, axis=-1, keepdims=True)
            y = cen * lax.rsqrt(var + GROUP_NORM_EPS) * gain
            z = z_ref[0, rows[u], :].astype(F32)
            o_ref[0, rows[u], :] = (y * (z * _sigmoid(z))).astype(o_ref.dtype)


def _gla(proj3, code3, w_alpha, b_alpha, gain, *, q_col, k_col, v_col, z_col, group):
    bsz, seq, _ = proj3.shape
    dk = w_alpha.shape[1] // GLA_HEADS
    dv = gain.shape[1] // GLA_HEADS
    assert seq % (GLA_CHUNK * group) == 0
    kernel = functools.partial(_gla_kernel, group=group)
    return pl.pallas_call(
        kernel,
        grid=(bsz, GLA_HEADS),
        in_specs=[
            pl.BlockSpec((1, seq, dk), lambda b, h: (b, 0, q_col // dk + h)),
            pl.BlockSpec((1, seq, dk), lambda b, h: (b, 0, k_col // dk + h)),
            pl.BlockSpec((1, seq, dv), lambda b, h: (b, 0, v_col // dv + h)),
            pl.BlockSpec((1, seq, dv), lambda b, h: (b, 0, z_col // dv + h)),
            pl.BlockSpec((1, seq, LANES), lambda b, h: (b, 0, 0)),
            pl.BlockSpec((LANES, dk), lambda b, h: (0, h)),
            pl.BlockSpec((1, dk), lambda b, h: (0, h)),
            pl.BlockSpec((1, dv), lambda b, h: (0, h)),
        ],
        out_specs=pl.BlockSpec((1, seq, dv), lambda b, h: (b, 0, h)),
        out_shape=jax.ShapeDtypeStruct((bsz, seq, GLA_HEADS * dv), BF16),
        scratch_shapes=[pltpu.VMEM((seq, dk), F32)],
        compiler_params=pltpu.CompilerParams(
            dimension_semantics=("parallel", "parallel"),
            vmem_limit_bytes=VMEM_LIMIT_BYTES),
        name="gla",
    )(proj3, proj3, proj3, proj3, code3, w_alpha, b_alpha, gain)


def _out_kernel(a1_ref, a2_ref, a3_ref, s1_ref, s2_ref, s3_ref, za_ref, ob_ref, ga_ref, gb_ref, x_ref,
                expand_ref, wa_ref, wb_ref, wo_ref, bgate_ref, fgain_ref, out_ref, *, chunks):
    tm, d = x_ref.shape
    rc = tm // chunks
    expand = expand_ref[...]
    bg = bgate_ref[...]
    head_lane = lax.broadcasted_iota(jnp.int32, (rc, LANES), 1) < ATTN_HEADS
    ya, yb = [], []
    for c in range(chunks):
        rows = slice(c * rc, (c + 1) * rc)
        yb.append(_dot(ob_ref[rows, :], wb_ref[...]))
        stats = [s_ref[rows, :] for s_ref in (s1_ref, s2_ref, s3_ref)]
        m = jnp.maximum(jnp.maximum(stats[0], stats[1]), stats[2])
        scale = [jnp.exp2(s - m) for s in stats]
        den = [pltpu.roll(s, LANES - DEN_LANE, 1) for s in stats]
        inv = 1.0 / (scale[0] * den[0] + scale[1] * den[1] + scale[2] * den[2])
        oa = None
        for p, a_ref in enumerate((a1_ref, a2_ref, a3_ref)):
            weight = jnp.where(head_lane, scale[p] * inv, 0.0).astype(BF16)
            term = _dot(weight, expand) * a_ref[rows, :].astype(F32)
            oa = term if oa is None else oa + term
        za = za_ref[rows, :].astype(F32)
        ya.append(_dot((oa * (za * _sigmoid(za))).astype(BF16), wa_ref[...]))
    h = []
    for c in range(chunks):
        rows = slice(c * rc, (c + 1) * rc)
        gate_a = _sigmoid(ga_ref[rows, :].astype(F32) + bg[:, :d])
        gate_b = _sigmoid(gb_ref[rows, :].astype(F32) + bg[:, d:])
        merged = (gate_a * ya[c] + gate_b * yb[c]).astype(BF16)
        h.append(x_ref[rows, :] + _dot(merged, wo_ref[...]))
    for c in range(chunks):
        rows = slice(c * rc, (c + 1) * rc)
        ms = jnp.mean(h[c] * h[c], axis=-1, keepdims=True)
        out_ref[rows, :] = h[c] * lax.rsqrt(ms + NORM_EPS) * fgain_ref[...]


def _out_stage(a_pats, s_pats, proj2, ob2, x2, expand, wa, wb, wo, bgate, fgain, *,
               tm, chunks, za_col, ga_col, gb_col):
    t, d = x2.shape
    row_blk = lambda cb: pl.BlockSpec((tm, d), lambda i, cb=cb: (i, cb))
    stat_blk = pl.BlockSpec((tm, LANES), lambda i: (i, 0))
    full = lambda a: pl.BlockSpec(a.shape, lambda i: (0, 0))
    return pl.pallas_call(
        functools.partial(_out_kernel, chunks=chunks),
        grid=(t // tm,),
        in_specs=[row_blk(0), row_blk(0), row_blk(0), stat_blk, stat_blk, stat_blk,
                  row_blk(za_col // d), row_blk(0), row_blk(ga_col // d), row_blk(gb_col // d), row_blk(0),
                  full(expand), full(wa), full(wb), full(wo), full(bgate), full(fgain)],
        out_specs=row_blk(0),
        out_shape=jax.ShapeDtypeStruct((t, d), F32),
        compiler_params=pltpu.CompilerParams(
            dimension_semantics=("parallel",),
            vmem_limit_bytes=VMEM_LIMIT_BYTES),
        name="out_stage",
    )(*a_pats, *s_pats, proj2, ob2, proj2, proj2, x2, expand, wa, wb, wo, bgate, fgain)


def kernel(x, norm_gain, w_in, b_gate, w_alpha, b_alpha, gla_norm_gain,
           w_out_attn, w_out_gla, w_out, final_norm_gain):
    bsz, seq, d = x.shape
    assert w_in.shape[0] == 1, "single-layer block"
    aw = ATTN_HEADS * ATTN_HEAD_DIM
    gk = w_alpha.shape[-1]
    gv = gla_norm_gain.shape[-1]
    sizes = (aw, aw, aw, aw, gk, gk, gv, gv, GLA_GATE_RANK, d, d)
    offs = [0]
    for s in sizes:
        offs.append(offs[-1] + s)
    assert offs[-1] == w_in.shape[-1]
    code_lo, code_hi = offs[8], offs[9]

    w_tail = w_in[0, :, code_hi:].astype(BF16)
    w_code = jnp.pad(w_in[0, :, code_lo:code_hi], ((0, 0), (0, LANES - GLA_GATE_RANK))).astype(BF16)
    w_alpha_p = jnp.pad(w_alpha[0], ((0, LANES - GLA_GATE_RANK), (0, 0)))
    slopes = 2.0 ** (-8.0 * (jnp.arange(ATTN_HEADS, dtype=F32) + 1.0) / ATTN_HEADS)
    head_of_col = jnp.arange(aw, dtype=jnp.int32) // ATTN_HEAD_DIM
    expand = (jnp.arange(LANES, dtype=jnp.int32)[:, None] == head_of_col[None, :]).astype(BF16)

    x2 = x.reshape(bsz * seq, d)
    proj, code = _in_proj(x2, norm_gain, w_in, code_lo, w_tail, w_code,
                          first_tile_scale=ATTN_HEAD_DIM ** -0.5 * LOG2E, tm=2048, tn=aw)
    proj3 = proj.reshape(bsz, seq, -1)
    rel = lambda o: o - (GLA_GATE_RANK if o > code_lo else 0)

    pats = [_attention_pattern(proj3, slopes, dilation=dil) for dil in DILATIONS]
    a_pats = [p[0] for p in pats]
    s_pats = [p[1] for p in pats]

    ob = _gla(proj3, code.reshape(bsz, seq, LANES), w_alpha_p, b_alpha, gla_norm_gain,
              q_col=rel(offs[4]), k_col=rel(offs[5]), v_col=rel(offs[6]), z_col=rel(offs[7]), group=8)

    out = _out_stage(a_pats, s_pats, proj, ob.reshape(bsz * seq, gv), x2, expand,
                     w_out_attn[0].astype(BF16), w_out_gla[0].astype(BF16), w_out[0].astype(BF16),
                     b_gate, final_norm_gain.reshape(1, d),
                     tm=512, chunks=2, za_col=rel(offs[3]), ga_col=rel(offs[9]), gb_col=rel(offs[10]))
    return out.reshape(bsz, seq, d)
```

```python
import functools
import math

import jax
import jax.numpy as jnp
from jax import lax
from jax.experimental import pallas as pl
from jax.experimental.pallas import tpu as pltpu

ATTN_HEADS = 16
ATTN_HEAD_DIM = 64
ATTN_WINDOW_STEPS = 128
DILATIONS = (1, 4, 16)
GLA_HEADS = 4
GLA_GATE_RANK = 16
GLA_GATE_TAU = 16.0
GLA_CHUNK = 64
NORM_EPS = 1e-6
GROUP_NORM_EPS = 1e-5

LANES = 128
VMEM_LIMIT_BYTES = 56 * 1024 * 1024

MASK_VALUE = -1e30
LOG2E = math.log2(math.e)
DEN_LANE = ATTN_HEADS
MOVE_ROWS = 512

F32 = jnp.float32
BF16 = jnp.bfloat16


def _dot(a, b):
    return jnp.dot(a, b, preferred_element_type=F32)


def _dot_nt(a, b):
    return lax.dot_general(a, b, (((1,), (1,)), ((), ())), preferred_element_type=F32)


def _dot_tn(a, b):
    return lax.dot_general(a, b, (((0,), (0,)), ((), ())), preferred_element_type=F32)


def _sigmoid(x):
    return 1.0 / (1.0 + jnp.exp(-x))


def _in_proj_kernel(x_ref, gain_ref, w_ref, wcode_ref, out_ref, code_ref, u_ref, *, first_tile_scale):
    j = pl.program_id(1)

    @pl.when(j == 0)
    def _():
        x = x_ref[...]
        ms = jnp.mean(x * x, axis=-1, keepdims=True)
        u = (x * lax.rsqrt(ms + NORM_EPS) * gain_ref[...]).astype(BF16)
        u_ref[...] = u
        code_ref[...] = _dot(u, wcode_ref[...])

    scale = jnp.where(j == 0, first_tile_scale, 1.0).astype(F32)
    out_ref[...] = (_dot(u_ref[...], w_ref[...]) * scale).astype(out_ref.dtype)


def _in_proj(x2, gain, w, w_code, *, first_tile_scale, tm, tn):
    t, d = x2.shape
    n = w.shape[1]
    return pl.pallas_call(
        functools.partial(_in_proj_kernel, first_tile_scale=first_tile_scale),
        grid=(t // tm, n // tn),
        in_specs=[
            pl.BlockSpec((tm, d), lambda i, j: (i, 0)),
            pl.BlockSpec((1, d), lambda i, j: (0, 0)),
            pl.BlockSpec((d, tn), lambda i, j: (0, j)),
            pl.BlockSpec((d, LANES), lambda i, j: (0, 0)),
        ],
        out_specs=[
            pl.BlockSpec((tm, tn), lambda i, j: (i, j)),
            pl.BlockSpec((tm, LANES), lambda i, j: (i, 0)),
        ],
        out_shape=[
            jax.ShapeDtypeStruct((t, n), BF16),
            jax.ShapeDtypeStruct((t, LANES), F32),
        ],
        scratch_shapes=[pltpu.VMEM((tm, d), BF16)],
        compiler_params=pltpu.CompilerParams(
            dimension_semantics=("parallel", "arbitrary"),
            vmem_limit_bytes=VMEM_LIMIT_BYTES),
        name="in_proj",
    )(x2, gain, w, w_code)


def _gather_items(src_ref, put, stage_ref, tmp_ref, dilation):
    assert dilation in (4, 16)
    seq = src_ref.shape[1]
    length = seq // dilation
    quarter = seq // 4
    pieces = quarter // MOVE_ROWS
    items = []

    def widen(rows):
        stage_ref[rows, :] = src_ref[0, rows, :].astype(F32)

    for c in range(seq // MOVE_ROWS):
        items.append((0, functools.partial(widen, slice(c * MOVE_ROWS, (c + 1) * MOVE_ROWS))))
    if dilation == 4:
        for r in range(4):
            for c in range(pieces):
                rows = slice(r * quarter + c * MOVE_ROWS, r * quarter + (c + 1) * MOVE_ROWS)
                src = pl.ds(r + 4 * c * MOVE_ROWS, MOVE_ROWS, stride=4)
                items.append((rows.start, lambda rows=rows, src=src: put(rows, stage_ref[src, :])))
        return items

    def to_tmp(rows, src):
        tmp_ref[rows, :] = stage_ref[src, :]

    for r4 in range(4):
        for c in range(pieces):
            rows = slice(r4 * quarter + c * MOVE_ROWS, r4 * quarter + (c + 1) * MOVE_ROWS)
            src = pl.ds(r4 + 4 * c * MOVE_ROWS, MOVE_ROWS, stride=4)
            items.append((r4 * length, functools.partial(to_tmp, rows, src)))
    for r4 in range(4):
        for q4 in range(4):
            r16 = r4 + 4 * q4
            rows = slice(r16 * length, (r16 + 1) * length)
            src = pl.ds(r4 * quarter + q4, length, stride=4)
            items.append((rows.start, lambda rows=rows, src=src: put(rows, tmp_ref[src, :])))
    return items


def _scatter_items(src_ref, tmp_ref, sink, dilation):
    seq = src_ref.shape[0]
    length = seq // dilation
    quarter = seq // 4
    pieces = quarter // MOVE_ROWS
    items = []

    def move(dst_ref, dst, from_ref, rows):
        dst_ref[dst, :] = from_ref[rows, :]

    final_ref = src_ref
    if dilation == 4:
        for r in range(4):
            for c in range(pieces):
                rows = slice(r * quarter + c * MOVE_ROWS, r * quarter + (c + 1) * MOVE_ROWS)
                dst = pl.ds(r + 4 * c * MOVE_ROWS, MOVE_ROWS, stride=4)
                items.append(functools.partial(move, tmp_ref, dst, src_ref, rows))
        final_ref = tmp_ref
    elif dilation == 16:
        for r4 in range(4):
            for q4 in range(4):
                r16 = r4 + 4 * q4
                rows = slice(r16 * length, (r16 + 1) * length)
                dst = pl.ds(r4 * quarter + q4, length, stride=4)
                items.append(functools.partial(move, tmp_ref, dst, src_ref, rows))
        for r4 in range(4):
            for c in range(pieces):
                rows = slice(r4 * quarter + c * MOVE_ROWS, r4 * quarter + (c + 1) * MOVE_ROWS)
                dst = pl.ds(r4 + 4 * c * MOVE_ROWS, MOVE_ROWS, stride=4)
                items.append(functools.partial(move, src_ref, dst, tmp_ref, rows))
    else:
        assert dilation == 1
    for c in range(seq // MOVE_ROWS):
        rows = slice(c * MOVE_ROWS, (c + 1) * MOVE_ROWS)
        items.append(lambda rows=rows: sink(rows, final_ref[rows, :]))
    return items


def _attn_kernel(slopes_ref, qa_ref, ka_ref, va_ref, qb_ref, kb_ref, vb_ref, o_ref, stats_ref,
                 qs_ref, ks_ref, vs_ref, stage_q_ref, stage_k_ref, stage_v_ref, gtmp_q_ref, gtmp_k_ref, gtmp_v_ref,
                 stmp_ref, acc0_ref, acc1_ref, st_ref, *, dilation):
    n = ATTN_WINDOW_STEPS
    seq = qa_ref.shape[1]
    n_blocks = seq // n
    blocks_per_class = seq // dilation // n
    step = pl.program_id(1)
    srcs = ((qa_ref, ka_ref, va_ref), (qb_ref, kb_ref, vb_ref))
    acc_refs = (acc0_ref, acc1_ref)
    stage_refs = (stage_q_ref, stage_k_ref, stage_v_ref)
    gtmp_refs = (gtmp_q_ref, gtmp_k_ref, gtmp_v_ref)

    @pl.when(step == 0)
    def _():
        st_ref[...] = jnp.zeros_like(st_ref)

    def put_q(group):
        def put(rows, vals):
            q = vals.astype(BF16)
            low = lax.broadcasted_iota(jnp.int32, q.shape, 1) < ATTN_HEAD_DIM
            zero = jnp.zeros_like(q)
            qs_ref[group, 0, rows, :] = jnp.where(low, q, zero)
            qs_ref[group, 1, rows, :] = jnp.where(low, zero, q)
        return put

    def put_plain(dst_ref, group):
        def put(rows, vals):
            dst_ref[group, rows, :] = vals.astype(BF16)
        return put

    def gather(group):
        q_src, k_src, v_src = srcs[group]
        if dilation == 1:
            put = put_q(group)
            return [(c * MOVE_ROWS, functools.partial(lambda rows: put(rows, q_src[0, rows, :]),
                                                      slice(c * MOVE_ROWS, (c + 1) * MOVE_ROWS)))
                    for c in range(seq // MOVE_ROWS)]
        items = (_gather_items(q_src, put_q(group), stage_refs[0], gtmp_refs[0], dilation)
                 + _gather_items(k_src, put_plain(ks_ref, group), stage_refs[1], gtmp_refs[1], dilation)
                 + _gather_items(v_src, put_plain(vs_ref, group), stage_refs[2], gtmp_refs[2], dilation))
        return sorted(items, key=lambda pair: pair[0])

    def k_rows(group, rows):
        return srcs[group][1][0, rows, :] if dilation == 1 else ks_ref[group, rows, :]

    def v_rows(group, rows):
        return srcs[group][2][0, rows, :] if dilation == 1 else vs_ref[group, rows, :]

    def scatter(group):
        def sink(rows, vals):
            o_ref[0, rows, group * LANES:(group + 1) * LANES] = vals.astype(o_ref.dtype)
        return _scatter_items(acc_refs[group], stmp_ref, sink, dilation)

    row = lax.broadcasted_iota(jnp.int32, (2 * n, 2 * n), 0)
    col = lax.broadcasted_iota(jnp.int32, (2 * n, 2 * n), 1)
    steps = n + (row % n) - col
    band = (steps >= 0) & (steps <= n)
    dist = (steps * dilation).astype(F32) * LOG2E
    lane = lax.broadcasted_iota(jnp.int32, (n, LANES), 1)
    low_half = lane < ATTN_HEAD_DIM

    def run_group(group, own_items, side_items):
        pair = 2 * step + group
        slope = jnp.where(row < n, slopes_ref[2 * pair], slopes_ref[2 * pair + 1])
        bias = jnp.where(band, -slope * dist, MASK_VALUE)
        bias_first = jnp.where(col < n, MASK_VALUE, bias)
        is_m0, is_m1 = lane == 2 * pair, lane == 2 * pair + 1
        is_d0, is_d1 = lane == DEN_LANE + 2 * pair, lane == DEN_LANE + 2 * pair + 1
        acc_ref = acc_refs[group]

        def block_rows(g):
            return slice(g * n, (g + 1) * n), slice(max(g - 1, 0) * n, max(g, 1) * n)

        def scores(g):
            rows, prev = block_rows(g)
            qq = jnp.concatenate([qs_ref[group, 0, rows, :], qs_ref[group, 1, rows, :]], axis=0)
            kk = jnp.concatenate([k_rows(group, prev), k_rows(group, rows)], axis=0)
            first = g % blocks_per_class == 0
            return _dot_nt(qq, kk) + (bias_first if first else bias)

        def finish(g, s):
            rows, prev = block_rows(g)
            m = jnp.max(s, axis=-1, keepdims=True)
            e = jnp.exp2(s - m)
            den = jnp.sum(e, axis=-1, keepdims=True)
            vv = jnp.concatenate([v_rows(group, prev), v_rows(group, rows)], axis=0)
            pv = _dot(e.astype(BF16), vv)
            acc_ref[rows, :] = jnp.where(low_half, pv[:n], pv[n:])
            cur = st_ref[rows, :]
            cur = jnp.where(is_m0, m[:n], jnp.where(is_m1, m[n:], cur))
            cur = jnp.where(is_d0, den[:n], jnp.where(is_d1, den[n:], cur))
            st_ref[rows, :] = cur

        own = list(own_items)
        side_start = n_blocks // 4 if own else 0

        def issue_own(upto_block):
            while own and own[0][0] < (upto_block + 1) * n:
                own.pop(0)[1]()

        issue_own(1)
        s_next = scores(0)
        for g in range(n_blocks):
            issue_own(4 * (g + 1) + 1)
            s_cur = s_next
            if g + 1 < n_blocks:
                s_next = scores(g + 1)
            finish(g, s_cur)
            if g >= side_start:
                k, span = g - side_start, n_blocks - side_start
                for item in side_items[k * len(side_items) // span:(k + 1) * len(side_items) // span]:
                    item()
        assert not own

    run_group(0, gather(0), [item for _, item in gather(1)])
    run_group(1, [], scatter(0))
    for item in scatter(1):
        item()

    @pl.when(step == pl.num_programs(1) - 1)
    def _():
        def sink(rows, vals):
            stats_ref[0, rows, :] = vals
        for item in _scatter_items(st_ref, stmp_ref, sink, dilation):
            item()


def _attention_pattern(proj3, slopes, *, dilation):
    bsz, seq, _ = proj3.shape
    width = ATTN_HEADS * ATTN_HEAD_DIM
    assert seq % (ATTN_WINDOW_STEPS * dilation) == 0 and seq % (4 * MOVE_ROWS) == 0
    kb = width // LANES

    def in_map(section, group):
        return lambda b, s, *_: (b, 0, section * kb + 2 * s + group)

    kernel = functools.partial(_attn_kernel, dilation=dilation)
    tile = lambda dtype: pltpu.VMEM((seq, LANES), dtype)
    small = pltpu.VMEM((8, LANES), F32)
    o, stats = pl.pallas_call(
        kernel,
        grid_spec=pltpu.PrefetchScalarGridSpec(
            num_scalar_prefetch=1,
            grid=(bsz, kb // 2),
            in_specs=[pl.BlockSpec((1, seq, LANES), in_map(section, group))
                      for group in range(2) for section in range(3)],
            out_specs=[
                pl.BlockSpec((1, seq, 2 * LANES), lambda b, s, *_: (b, 0, s)),
                pl.BlockSpec((1, seq, LANES), lambda b, s, *_: (b, 0, 0)),
            ],
            scratch_shapes=[
                pltpu.VMEM((2, 2, seq, LANES), BF16),
                pltpu.VMEM((2, seq, LANES), BF16),
                pltpu.VMEM((2, seq, LANES), BF16),
                *[tile(F32) if dilation > 1 else small for _ in range(3)],
                *[tile(F32) if dilation == 16 else small for _ in range(3)],
                tile(F32),
                tile(F32),
                tile(F32),
                tile(F32),
            ],
        ),
        out_shape=[
            jax.ShapeDtypeStruct((bsz, seq, width), BF16),
            jax.ShapeDtypeStruct((bsz, seq, LANES), F32),
        ],
        compiler_params=pltpu.CompilerParams(
            dimension_semantics=("parallel", "arbitrary"),
            vmem_limit_bytes=VMEM_LIMIT_BYTES),
        name=f"dilated_attn_d{dilation}",
    )(slopes, *([proj3] * 6))
    return o.reshape(bsz * seq, width), stats.reshape(bsz * seq, LANES)


def _split_bf16(x):
    hi = x.astype(BF16)
    return hi, (x - hi.astype(F32)).astype(BF16)


def _gla_kernel(q_ref, k_ref, v_ref, z_ref, code_ref, walpha_ref, balpha_ref, gain_ref,
                o_ref, b_ref, *, group):
    c = GLA_CHUNK
    seq, dk = q_ref.shape[1], q_ref.shape[2]
    dv = v_ref.shape[2]
    n_groups = seq // (c * group)

    ri = lax.broadcasted_iota(jnp.int32, (c, c), 0)
    ci = lax.broadcasted_iota(jnp.int32, (c, c), 1)
    causal = ri >= ci
    tri = causal.astype(BF16)
    gain = gain_ref[...]
    w_hi, w_lo = _split_bf16(walpha_ref[...])
    b_alpha = balpha_ref[...]

    def chunk_rows(g, u):
        return slice((g * group + u) * c, (g * group + u + 1) * c)

    def gate_logs(g):
        rows = slice(g * group * c, (g + 1) * group * c)
        code_hi, code_lo = _split_bf16(code_ref[0, rows, :])
        logits = _dot(code_hi, w_hi) + _dot(code_hi, w_lo) + _dot(code_lo, w_hi) + b_alpha
        log_sig = jnp.minimum(logits, 0.0) - jnp.log(1.0 + jnp.exp(-jnp.abs(logits)))
        b_ref[rows, :] = log_sig * (LOG2E / GLA_GATE_TAU)

    def cumsums(g):
        for u in range(group):
            rows = chunk_rows(g, u)
            la = b_ref[rows, :]
            la1 = la.astype(BF16)
            rem = la - la1.astype(F32)
            la2 = rem.astype(BF16)
            la3 = (rem - la2.astype(F32)).astype(BF16)
            b_ref[rows, :] = _dot(tri, la1) + _dot(tri, la2) + _dot(tri, la3)

    gate_logs(0)
    cumsums(0)
    st = jnp.zeros((dv, dk), F32)
    for g in range(n_groups):
        rows = [chunk_rows(g, u) for u in range(group)]
        q_dec, k_inv, k_end, decay, v = [], [], [], [], []
        for u in range(group):
            b = b_ref[rows[u], :]
            b_last = b[c - 1:c, :]
            q = q_ref[0, rows[u], :].astype(F32) * (dk ** -0.5)
            k = k_ref[0, rows[u], :].astype(F32)
            q_dec.append((q * jnp.exp2(b)).astype(BF16))
            k_undecayed = k * jnp.exp2(-b)
            decay.append(jnp.exp2(b_last))
            k_inv.append(k_undecayed.astype(BF16))
            k_end.append((k_undecayed * decay[u]).astype(BF16))
            v.append(v_ref[0, rows[u], :])
        a = [jnp.where(causal, _dot_nt(q_dec[u], k_inv[u]), 0.0).astype(BF16) for u in range(group)]
        d_state = [_dot_tn(v[u], k_end[u]) for u in range(group)]
        o_intra = [_dot(a[u], v[u]) for u in range(group)]
        if g + 1 < n_groups:
            gate_logs(g + 1)
        o = []
        for u in range(group):
            o.append(o_intra[u] + _dot_nt(q_dec[u], st.astype(BF16)))
            st = decay[u] * st + d_state[u]
        if g + 1 < n_groups:
            cumsums(g + 1)
        for u in range(group):
            mu = jnp.mean(o[u], axis=-1, keepdims=True)
            cen = o[u] - mu
            var = jnp.mean(cen * cen, axis=-1, keepdims=True)
            y = cen * lax.rsqrt(var + GROUP_NORM_EPS) * gain
            z = z_ref[0, rows[u], :].astype(F32)
            o_ref[0, rows[u], :] = (y * (z * _sigmoid(z))).astype(o_ref.dtype)


def _gla(proj3, code3, w_alpha, b_alpha, gain, *, q_col, k_col, v_col, z_col, group):
    bsz, seq, _ = proj3.shape
    dk = w_alpha.shape[1] // GLA_HEADS
    dv = gain.shape[1] // GLA_HEADS
    assert seq % (GLA_CHUNK * group) == 0
    kernel = functools.partial(_gla_kernel, group=group)
    return pl.pallas_call(
        kernel,
        grid=(bsz, GLA_HEADS),
        in_specs=[
            pl.BlockSpec((1, seq, dk), lambda b, h: (b, 0, q_col // dk + h)),
            pl.BlockSpec((1, seq, dk), lambda b, h: (b, 0, k_col // dk + h)),
            pl.BlockSpec((1, seq, dv), lambda b, h: (b, 0, v_col // dv + h)),
            pl.BlockSpec((1, seq, dv), lambda b, h: (b, 0, z_col // dv + h)),
            pl.BlockSpec((1, seq, LANES), lambda b, h: (b, 0, 0)),
            pl.BlockSpec((LANES, dk), lambda b, h: (0, h)),
            pl.BlockSpec((1, dk), lambda b, h: (0, h)),
            pl.BlockSpec((1, dv), lambda b, h: (0, h)),
        ],
        out_specs=pl.BlockSpec((1, seq, dv), lambda b, h: (b, 0, h)),
        out_shape=jax.ShapeDtypeStruct((bsz, seq, GLA_HEADS * dv), BF16),
        scratch_shapes=[pltpu.VMEM((seq, dk), F32)],
        compiler_params=pltpu.CompilerParams(
            dimension_semantics=("parallel", "parallel"),
            vmem_limit_bytes=VMEM_LIMIT_BYTES),
        name="gla",
    )(proj3, proj3, proj3, proj3, code3, w_alpha, b_alpha, gain)


def _out_kernel(a1_ref, a2_ref, a3_ref, s1_ref, s2_ref, s3_ref, za_ref, ob_ref, ga_ref, gb_ref, x_ref,
                expand_ref, wa_ref, wb_ref, wo_ref, bgate_ref, fgain_ref, out_ref, *, chunks):
    tm, d = x_ref.shape
    rc = tm // chunks
    expand = expand_ref[...]
    bg = bgate_ref[...]
    head_lane = lax.broadcasted_iota(jnp.int32, (rc, LANES), 1) < ATTN_HEADS
    ya, yb = [], []
    for c in range(chunks):
        rows = slice(c * rc, (c + 1) * rc)
        yb.append(_dot(ob_ref[rows, :], wb_ref[...]))
        stats = [s_ref[rows, :] for s_ref in (s1_ref, s2_ref, s3_ref)]
        m = jnp.maximum(jnp.maximum(stats[0], stats[1]), stats[2])
        scale = [jnp.exp2(s - m) for s in stats]
        den = [pltpu.roll(s, LANES - DEN_LANE, 1) for s in stats]
        inv = 1.0 / (scale[0] * den[0] + scale[1] * den[1] + scale[2] * den[2])
        oa = None
        for p, a_ref in enumerate((a1_ref, a2_ref, a3_ref)):
            weight = jnp.where(head_lane, scale[p] * inv, 0.0).astype(BF16)
            term = _dot(weight, expand) * a_ref[rows, :].astype(F32)
            oa = term if oa is None else oa + term
        za = za_ref[rows, :].astype(F32)
        ya.append(_dot((oa * (za * _sigmoid(za))).astype(BF16), wa_ref[...]))
    h = []
    for c in range(chunks):
        rows = slice(c * rc, (c + 1) * rc)
        gate_a = _sigmoid(ga_ref[rows, :].astype(F32) + bg[:, :d])
        gate_b = _sigmoid(gb_ref[rows, :].astype(F32) + bg[:, d:])
        merged = (gate_a * ya[c] + gate_b * yb[c]).astype(BF16)
        h.append(x_ref[rows, :] + _dot(merged, wo_ref[...]))
    for c in range(chunks):
        rows = slice(c * rc, (c + 1) * rc)
        ms = jnp.mean(h[c] * h[c], axis=-1, keepdims=True)
        out_ref[rows, :] = h[c] * lax.rsqrt(ms + NORM_EPS) * fgain_ref[...]


def _out_stage(a_pats, s_pats, proj2, ob2, x2, expand, wa, wb, wo, bgate, fgain, *,
               tm, chunks, za_col, ga_col, gb_col):
    t, d = x2.shape
    row_blk = lambda cb: pl.BlockSpec((tm, d), lambda i, cb=cb: (i, cb))
    stat_blk = pl.BlockSpec((tm, LANES), lambda i: (i, 0))
    full = lambda a: pl.BlockSpec(a.shape, lambda i: (0, 0))
    return pl.pallas_call(
        functools.partial(_out_kernel, chunks=chunks),
        grid=(t // tm,),
        in_specs=[row_blk(0), row_blk(0), row_blk(0), stat_blk, stat_blk, stat_blk,
                  row_blk(za_col // d), row_blk(0), row_blk(ga_col // d), row_blk(gb_col // d), row_blk(0),
                  full(expand), full(wa), full(wb), full(wo), full(bgate), full(fgain)],
        out_specs=row_blk(0),
        out_shape=jax.ShapeDtypeStruct((t, d), F32),
        compiler_params=pltpu.CompilerParams(
            dimension_semantics=("parallel",),
            vmem_limit_bytes=VMEM_LIMIT_BYTES),
        name="out_stage",
    )(*a_pats, *s_pats, proj2, ob2, proj2, proj2, x2, expand, wa, wb, wo, bgate, fgain)


def kernel(x, norm_gain, w_in, b_gate, w_alpha, b_alpha, gla_norm_gain,
           w_out_attn, w_out_gla, w_out, final_norm_gain):
    bsz, seq, d = x.shape
    assert w_in.shape[0] == 1, "single-layer block"
    aw = ATTN_HEADS * ATTN_HEAD_DIM
    gk = w_alpha.shape[-1]
    gv = gla_norm_gain.shape[-1]
    sizes = (aw, aw, aw, aw, gk, gk, gv, gv, GLA_GATE_RANK, d, d)
    offs = [0]
    for s in sizes:
        offs.append(offs[-1] + s)
    assert offs[-1] == w_in.shape[-1]
    code_lo, code_hi = offs[8], offs[9]

    w = w_in[0].astype(BF16)
    w_main = jnp.concatenate([w[:, :code_lo], w[:, code_hi:]], axis=1)
    w_code = jnp.pad(w[:, code_lo:code_hi], ((0, 0), (0, LANES - GLA_GATE_RANK)))
    w_alpha_p = jnp.pad(w_alpha[0], ((0, LANES - GLA_GATE_RANK), (0, 0)))
    slopes = 2.0 ** (-8.0 * (jnp.arange(ATTN_HEADS, dtype=F32) + 1.0) / ATTN_HEADS)
    head_of_col = jnp.arange(aw, dtype=jnp.int32) // ATTN_HEAD_DIM
    expand = (jnp.arange(LANES, dtype=jnp.int32)[:, None] == head_of_col[None, :]).astype(BF16)

    x2 = x.reshape(bsz * seq, d)
    proj, code = _in_proj(x2, norm_gain, w_main, w_code, first_tile_scale=ATTN_HEAD_DIM ** -0.5 * LOG2E,
                          tm=2048, tn=aw)
    proj3 = proj.reshape(bsz, seq, -1)
    rel = lambda o: o - (GLA_GATE_RANK if o > code_lo else 0)

    pats = [_attention_pattern(proj3, slopes, dilation=dil) for dil in DILATIONS]
    a_pats = [p[0] for p in pats]
    s_pats = [p[1] for p in pats]

    ob = _gla(proj3, code.reshape(bsz, seq, LANES), w_alpha_p, b_alpha, gla_norm_gain,
              q_col=rel(offs[4]), k_col=rel(offs[5]), v_col=rel(offs[6]), z_col=rel(offs[7]), group=8)

    out = _out_stage(a_pats, s_pats, proj, ob.reshape(bsz * seq, gv), x2, expand,
                     w_out_attn[0].astype(BF16), w_out_gla[0].astype(BF16), w_out[0].astype(BF16),
                     b_gate, final_norm_gain.reshape(1, d),
                     tm=512, chunks=2, za_col=rel(offs[3]), ga_col=rel(offs[9]), gb_col=rel(offs[10]))
    return out.reshape(bsz, seq, d)
```

```python
import functools
import math

import jax
import jax.numpy as jnp
from jax import lax
from jax.experimental import pallas as pl
from jax.experimental.pallas import tpu as pltpu

ATTN_HEADS = 16
ATTN_HEAD_DIM = 64
ATTN_WINDOW_STEPS = 128
DILATIONS = (1, 4, 16)
GLA_HEADS = 4
GLA_GATE_RANK = 16
GLA_GATE_TAU = 16.0
GLA_CHUNK = 64
NORM_EPS = 1e-6
GROUP_NORM_EPS = 1e-5

LANES = 128
VMEM_LIMIT_BYTES = 56 * 1024 * 1024

MASK_VALUE = -1e30
LOG2E = math.log2(math.e)
DEN_LANE = ATTN_HEADS
MOVE_ROWS = 512

F32 = jnp.float32
BF16 = jnp.bfloat16


def _dot(a, b):
    return jnp.dot(a, b, preferred_element_type=F32)


def _dot_nt(a, b):
    return lax.dot_general(a, b, (((1,), (1,)), ((), ())), preferred_element_type=F32)


def _dot_tn(a, b):
    return lax.dot_general(a, b, (((0,), (0,)), ((), ())), preferred_element_type=F32)


def _sigmoid(x):
    return 1.0 / (1.0 + jnp.exp(-x))


def _in_proj_kernel(x_ref, gain_ref, wf_ref, wt_ref, wcode_ref, out_ref, code_ref, u_ref, *,
                    n_front, first_tile_scale):
    j = pl.program_id(1)

    @pl.when(j == 0)
    def _():
        x = x_ref[...]
        ms = jnp.mean(x * x, axis=-1, keepdims=True)
        u = (x * lax.rsqrt(ms + NORM_EPS) * gain_ref[...]).astype(BF16)
        u_ref[...] = u
        code_ref[...] = _dot(u, wcode_ref[...])

    @pl.when(j < n_front)
    def _():
        scale = jnp.where(j == 0, first_tile_scale, 1.0).astype(F32)
        out_ref[...] = (_dot(u_ref[...], wf_ref[...]) * scale).astype(out_ref.dtype)

    @pl.when(j >= n_front)
    def _():
        out_ref[...] = _dot(u_ref[...], wt_ref[...]).astype(out_ref.dtype)


def _in_proj(x2, gain, w_front, w_tail, w_code, *, first_tile_scale, tm, tn):
    t, d = x2.shape
    n_front = w_front.shape[1] // tn
    n = w_front.shape[1] + w_tail.shape[1]
    assert w_front.shape[1] % tn == 0 and w_tail.shape[1] % tn == 0
    return pl.pallas_call(
        functools.partial(_in_proj_kernel, n_front=n_front, first_tile_scale=first_tile_scale),
        grid=(t // tm, n // tn),
        in_specs=[
            pl.BlockSpec((tm, d), lambda i, j: (i, 0)),
            pl.BlockSpec((1, d), lambda i, j: (0, 0)),
            pl.BlockSpec((d, tn), lambda i, j: (0, jnp.minimum(j, n_front - 1))),
            pl.BlockSpec((d, tn), lambda i, j: (0, jnp.maximum(j - n_front, 0))),
            pl.BlockSpec((d, LANES), lambda i, j: (0, 0)),
        ],
        out_specs=[
            pl.BlockSpec((tm, tn), lambda i, j: (i, j)),
            pl.BlockSpec((tm, LANES), lambda i, j: (i, 0)),
        ],
        out_shape=[
            jax.ShapeDtypeStruct((t, n), BF16),
            jax.ShapeDtypeStruct((t, LANES), F32),
        ],
        scratch_shapes=[pltpu.VMEM((tm, d), BF16)],
        compiler_params=pltpu.CompilerParams(
            dimension_semantics=("parallel", "arbitrary"),
            vmem_limit_bytes=VMEM_LIMIT_BYTES),
        name="in_proj",
    )(x2, gain, w_front, w_tail, w_code)


def _gather_items(src_ref, put, stage_ref, tmp_ref, dilation):
    assert dilation in (4, 16)
    seq = src_ref.shape[1]
    length = seq // dilation
    quarter = seq // 4
    pieces = quarter // MOVE_ROWS
    items = []

    def widen(rows):
        stage_ref[rows, :] = src_ref[0, rows, :].astype(F32)

    for c in range(seq // MOVE_ROWS):
        items.append((0, functools.partial(widen, slice(c * MOVE_ROWS, (c + 1) * MOVE_ROWS))))
    if dilation == 4:
        for r in range(4):
            for c in range(pieces):
                rows = slice(r * quarter + c * MOVE_ROWS, r * quarter + (c + 1) * MOVE_ROWS)
                src = pl.ds(r + 4 * c * MOVE_ROWS, MOVE_ROWS, stride=4)
                items.append((rows.start, lambda rows=rows, src=src: put(rows, stage_ref[src, :])))
        return items

    def to_tmp(rows, src):
        tmp_ref[rows, :] = stage_ref[src, :]

    for r4 in range(4):
        for c in range(pieces):
            rows = slice(r4 * quarter + c * MOVE_ROWS, r4 * quarter + (c + 1) * MOVE_ROWS)
            src = pl.ds(r4 + 4 * c * MOVE_ROWS, MOVE_ROWS, stride=4)
            items.append((r4 * length, functools.partial(to_tmp, rows, src)))
    for r4 in range(4):
        for q4 in range(4):
            r16 = r4 + 4 * q4
            rows = slice(r16 * length, (r16 + 1) * length)
            src = pl.ds(r4 * quarter + q4, length, stride=4)
            items.append((rows.start, lambda rows=rows, src=src: put(rows, tmp_ref[src, :])))
    return items


def _scatter_items(src_ref, tmp_ref, sink, dilation):
    seq = src_ref.shape[0]
    length = seq // dilation
    quarter = seq // 4
    pieces = quarter // MOVE_ROWS
    items = []

    def move(dst_ref, dst, from_ref, rows):
        dst_ref[dst, :] = from_ref[rows, :]

    final_ref = src_ref
    if dilation == 4:
        for r in range(4):
            for c in range(pieces):
                rows = slice(r * quarter + c * MOVE_ROWS, r * quarter + (c + 1) * MOVE_ROWS)
                dst = pl.ds(r + 4 * c * MOVE_ROWS, MOVE_ROWS, stride=4)
                items.append(functools.partial(move, tmp_ref, dst, src_ref, rows))
        final_ref = tmp_ref
    elif dilation == 16:
        for r4 in range(4):
            for q4 in range(4):
                r16 = r4 + 4 * q4
                rows = slice(r16 * length, (r16 + 1) * length)
                dst = pl.ds(r4 * quarter + q4, length, stride=4)
                items.append(functools.partial(move, tmp_ref, dst, src_ref, rows))
        for r4 in range(4):
            for c in range(pieces):
                rows = slice(r4 * quarter + c * MOVE_ROWS, r4 * quarter + (c + 1) * MOVE_ROWS)
                dst = pl.ds(r4 + 4 * c * MOVE_ROWS, MOVE_ROWS, stride=4)
                items.append(functools.partial(move, src_ref, dst, tmp_ref, rows))
    else:
        assert dilation == 1
    for c in range(seq // MOVE_ROWS):
        rows = slice(c * MOVE_ROWS, (c + 1) * MOVE_ROWS)
        items.append(lambda rows=rows: sink(rows, final_ref[rows, :]))
    return items


def _attn_kernel(slopes_ref, qa_ref, ka_ref, va_ref, qb_ref, kb_ref, vb_ref, o_ref, stats_ref,
                 qs_ref, ks_ref, vs_ref, stage_q_ref, stage_k_ref, stage_v_ref, gtmp_q_ref, gtmp_k_ref, gtmp_v_ref,
                 stmp_ref, acc0_ref, acc1_ref, st_ref, *, dilation):
    n = ATTN_WINDOW_STEPS
    seq = qa_ref.shape[1]
    n_blocks = seq // n
    blocks_per_class = seq // dilation // n
    step = pl.program_id(1)
    srcs = ((qa_ref, ka_ref, va_ref), (qb_ref, kb_ref, vb_ref))
    acc_refs = (acc0_ref, acc1_ref)
    stage_refs = (stage_q_ref, stage_k_ref, stage_v_ref)
    gtmp_refs = (gtmp_q_ref, gtmp_k_ref, gtmp_v_ref)

    @pl.when(step == 0)
    def _():
        st_ref[...] = jnp.zeros_like(st_ref)

    def put_q(group):
        def put(rows, vals):
            q = vals.astype(BF16)
            low = lax.broadcasted_iota(jnp.int32, q.shape, 1) < ATTN_HEAD_DIM
            zero = jnp.zeros_like(q)
            qs_ref[group, 0, rows, :] = jnp.where(low, q, zero)
            qs_ref[group, 1, rows, :] = jnp.where(low, zero, q)
        return put

    def put_plain(dst_ref, group):
        def put(rows, vals):
            dst_ref[group, rows, :] = vals.astype(BF16)
        return put

    def gather(group):
        q_src, k_src, v_src = srcs[group]
        if dilation == 1:
            put = put_q(group)
            return [(c * MOVE_ROWS, functools.partial(lambda rows: put(rows, q_src[0, rows, :]),
                                                      slice(c * MOVE_ROWS, (c + 1) * MOVE_ROWS)))
                    for c in range(seq // MOVE_ROWS)]
        items = (_gather_items(q_src, put_q(group), stage_refs[0], gtmp_refs[0], dilation)
                 + _gather_items(k_src, put_plain(ks_ref, group), stage_refs[1], gtmp_refs[1], dilation)
                 + _gather_items(v_src, put_plain(vs_ref, group), stage_refs[2], gtmp_refs[2], dilation))
        return sorted(items, key=lambda pair: pair[0])

    def k_rows(group, rows):
        return srcs[group][1][0, rows, :] if dilation == 1 else ks_ref[group, rows, :]

    def v_rows(group, rows):
        return srcs[group][2][0, rows, :] if dilation == 1 else vs_ref[group, rows, :]

    def scatter(group):
        def sink(rows, vals):
            o_ref[0, rows, group * LANES:(group + 1) * LANES] = vals.astype(o_ref.dtype)
        return _scatter_items(acc_refs[group], stmp_ref, sink, dilation)

    row = lax.broadcasted_iota(jnp.int32, (2 * n, 2 * n), 0)
    col = lax.broadcasted_iota(jnp.int32, (2 * n, 2 * n), 1)
    steps = n + (row % n) - col
    band = (steps >= 0) & (steps <= n)
    dist = (steps * dilation).astype(F32) * LOG2E
    lane = lax.broadcasted_iota(jnp.int32, (n, LANES), 1)
    low_half = lane < ATTN_HEAD_DIM

    def run_group(group, own_items, side_items):
        pair = 2 * step + group
        slope = jnp.where(row < n, slopes_ref[2 * pair], slopes_ref[2 * pair + 1])
        bias = jnp.where(band, -slope * dist, MASK_VALUE)
        bias_first = jnp.where(col < n, MASK_VALUE, bias)
        is_m0, is_m1 = lane == 2 * pair, lane == 2 * pair + 1
        is_d0, is_d1 = lane == DEN_LANE + 2 * pair, lane == DEN_LANE + 2 * pair + 1
        acc_ref = acc_refs[group]

        def block_rows(g):
            return slice(g * n, (g + 1) * n), slice(max(g - 1, 0) * n, max(g, 1) * n)

        def scores(g):
            rows, prev = block_rows(g)
            qq = jnp.concatenate([qs_ref[group, 0, rows, :], qs_ref[group, 1, rows, :]], axis=0)
            kk = jnp.concatenate([k_rows(group, prev), k_rows(group, rows)], axis=0)
            first = g % blocks_per_class == 0
            return _dot_nt(qq, kk) + (bias_first if first else bias)

        def finish(g, s):
            rows, prev = block_rows(g)
            m = jnp.max(s, axis=-1, keepdims=True)
            e = jnp.exp2(s - m)
            den = jnp.sum(e, axis=-1, keepdims=True)
            vv = jnp.concatenate([v_rows(group, prev), v_rows(group, rows)], axis=0)
            pv = _dot(e.astype(BF16), vv)
            acc_ref[rows, :] = jnp.where(low_half, pv[:n], pv[n:])
            cur = st_ref[rows, :]
            cur = jnp.where(is_m0, m[:n], jnp.where(is_m1, m[n:], cur))
            cur = jnp.where(is_d0, den[:n], jnp.where(is_d1, den[n:], cur))
            st_ref[rows, :] = cur

        own = list(own_items)
        side_start = n_blocks // 4 if own else 0

        def issue_own(upto_block):
            while own and own[0][0] < (upto_block + 1) * n:
                own.pop(0)[1]()

        issue_own(1)
        s_next = scores(0)
        for g in range(n_blocks):
            issue_own(4 * (g + 1) + 1)
            s_cur = s_next
            if g + 1 < n_blocks:
                s_next = scores(g + 1)
            finish(g, s_cur)
            if g >= side_start:
                k, span = g - side_start, n_blocks - side_start
                for item in side_items[k * len(side_items) // span:(k + 1) * len(side_items) // span]:
                    item()
        assert not own

    run_group(0, gather(0), [item for _, item in gather(1)])
    run_group(1, [], scatter(0))
    for item in scatter(1):
        item()

    @pl.when(step == pl.num_programs(1) - 1)
    def _():
        def sink(rows, vals):
            stats_ref[0, rows, :] = vals
        for item in _scatter_items(st_ref, stmp_ref, sink, dilation):
            item()


def _attention_pattern(proj3, slopes, *, dilation):
    bsz, seq, _ = proj3.shape
    width = ATTN_HEADS * ATTN_HEAD_DIM
    assert seq % (ATTN_WINDOW_STEPS * dilation) == 0 and seq % (4 * MOVE_ROWS) == 0
    kb = width // LANES

    def in_map(section, group):
        return lambda b, s, *_: (b, 0, section * kb + 2 * s + group)

    kernel = functools.partial(_attn_kernel, dilation=dilation)
    tile = lambda dtype: pltpu.VMEM((seq, LANES), dtype)
    small = pltpu.VMEM((8, LANES), F32)
    o, stats = pl.pallas_call(
        kernel,
        grid_spec=pltpu.PrefetchScalarGridSpec(
            num_scalar_prefetch=1,
            grid=(bsz, kb // 2),
            in_specs=[pl.BlockSpec((1, seq, LANES), in_map(section, group))
                      for group in range(2) for section in range(3)],
            out_specs=[
                pl.BlockSpec((1, seq, 2 * LANES), lambda b, s, *_: (b, 0, s)),
                pl.BlockSpec((1, seq, LANES), lambda b, s, *_: (b, 0, 0)),
            ],
            scratch_shapes=[
                pltpu.VMEM((2, 2, seq, LANES), BF16),
                pltpu.VMEM((2, seq, LANES), BF16),
                pltpu.VMEM((2, seq, LANES), BF16),
                *[tile(F32) if dilation > 1 else small for _ in range(3)],
                *[tile(F32) if dilation == 16 else small for _ in range(3)],
                tile(F32),
                tile(F32),
                tile(F32),
                tile(F32),
            ],
        ),
        out_shape=[
            jax.ShapeDtypeStruct((bsz, seq, width), BF16),
            jax.ShapeDtypeStruct((bsz, seq, LANES), F32),
        ],
        compiler_params=pltpu.CompilerParams(
            dimension_semantics=("parallel", "arbitrary"),
            vmem_limit_bytes=VMEM_LIMIT_BYTES),
        name=f"dilated_attn_d{dilation}",
    )(slopes, *([proj3] * 6))
    return o.reshape(bsz * seq, width), stats.reshape(bsz * seq, LANES)


def _split_bf16(x):
    hi = x.astype(BF16)
    return hi, (x - hi.astype(F32)).astype(BF16)


def _gla_kernel(q_ref, k_ref, v_ref, z_ref, code_ref, walpha_ref, balpha_ref, gain_ref,
                o_ref, b_ref, *, group):
    c = GLA_CHUNK
    seq, dk = q_ref.shape[1], q_ref.shape[2]
    dv = v_ref.shape[2]
    n_groups = seq // (c * group)

    ri = lax.broadcasted_iota(jnp.int32, (c, c), 0)
    ci = lax.broadcasted_iota(jnp.int32, (c, c), 1)
    causal = ri >= ci
    tri = causal.astype(BF16)
    gain = gain_ref[...]
    w_hi, w_lo = _split_bf16(walpha_ref[...])
    b_alpha = balpha_ref[...]

    def chunk_rows(g, u):
        return slice((g * group + u) * c, (g * group + u + 1) * c)

    def gate_logs(g):
        rows = slice(g * group * c, (g + 1) * group * c)
        code_hi, code_lo = _split_bf16(code_ref[0, rows, :])
        logits = _dot(code_hi, w_hi) + _dot(code_hi, w_lo) + _dot(code_lo, w_hi) + b_alpha
        log_sig = jnp.minimum(logits, 0.0) - jnp.log(1.0 + jnp.exp(-jnp.abs(logits)))
        b_ref[rows, :] = log_sig * (LOG2E / GLA_GATE_TAU)

    def cumsums(g):
        for u in range(group):
            rows = chunk_rows(g, u)
            la = b_ref[rows, :]
            la1 = la.astype(BF16)
            rem = la - la1.astype(F32)
            la2 = rem.astype(BF16)
            la3 = (rem - la2.astype(F32)).astype(BF16)
            b_ref[rows, :] = _dot(tri, la1) + _dot(tri, la2) + _dot(tri, la3)

    gate_logs(0)
    cumsums(0)
    st = jnp.zeros((dv, dk), F32)
    for g in range(n_groups):
        rows = [chunk_rows(g, u) for u in range(group)]
        q_dec, k_inv, k_end, decay, v = [], [], [], [], []
        for u in range(group):
            b = b_ref[rows[u], :]
            b_last = b[c - 1:c, :]
            q = q_ref[0, rows[u], :].astype(F32) * (dk ** -0.5)
            k = k_ref[0, rows[u], :].astype(F32)
            q_dec.append((q * jnp.exp2(b)).astype(BF16))
            k_undecayed = k * jnp.exp2(-b)
            decay.append(jnp.exp2(b_last))
            k_inv.append(k_undecayed.astype(BF16))
            k_end.append((k_undecayed * decay[u]).astype(BF16))
            v.append(v_ref[0, rows[u], :])
        a = [jnp.where(causal, _dot_nt(q_dec[u], k_inv[u]), 0.0).astype(BF16) for u in range(group)]
        d_state = [_dot_tn(v[u], k_end[u]) for u in range(group)]
        o_intra = [_dot(a[u], v[u]) for u in range(group)]
        if g + 1 < n_groups:
            gate_logs(g + 1)
        o = []
        for u in range(group):
            o.append(o_intra[u] + _dot_nt(q_dec[u], st.astype(BF16)))
            st = decay[u] * st + d_state[u]
        if g + 1 < n_groups:
            cumsums(g + 1)
        for u in range(group):
            mu = jnp.mean(o[u], axis=-1, keepdims=True)
            cen = o[u] - mu
            var = jnp.mean(cen * cen, axis=-1, keepdims=True)
            y = cen * lax.rsqrt(var + GROUP_NORM_EPS) * gain
            z = z_ref[0, rows[u], :].astype(F32)
            o_ref[0, rows[u], :] = (y * (z * _sigmoid(z))).astype(o_ref.dtype)


def _gla(proj3, code3, w_alpha, b_alpha, gain, *, q_col, k_col, v_col, z_col, group):
    bsz, seq, _ = proj3.shape
    dk = w_alpha.shape[1] // GLA_HEADS
    dv = gain.shape[1] // GLA_HEADS
    assert seq % (GLA_CHUNK * group) == 0
    kernel = functools.partial(_gla_kernel, group=group)
    return pl.pallas_call(
        kernel,
        grid=(bsz, GLA_HEADS),
        in_specs=[
            pl.BlockSpec((1, seq, dk), lambda b, h: (b, 0, q_col // dk + h)),
            pl.BlockSpec((1, seq, dk), lambda b, h: (b, 0, k_col // dk + h)),
            pl.BlockSpec((1, seq, dv), lambda b, h: (b, 0, v_col // dv + h)),
            pl.BlockSpec((1, seq, dv), lambda b, h: (b, 0, z_col // dv + h)),
            pl.BlockSpec((1, seq, LANES), lambda b, h: (b, 0, 0)),
            pl.BlockSpec((LANES, dk), lambda b, h: (0, h)),
            pl.BlockSpec((1, dk), lambda b, h: (0, h)),
            pl.BlockSpec((1, dv), lambda b, h: (0, h)),
        ],
        out_specs=pl.BlockSpec((1, seq, dv), lambda b, h: (b, 0, h)),
        out_shape=jax.ShapeDtypeStruct((bsz, seq, GLA_HEADS * dv), BF16),
        scratch_shapes=[pltpu.VMEM((seq, dk), F32)],
        compiler_params=pltpu.CompilerParams(
            dimension_semantics=("parallel", "parallel"),
            vmem_limit_bytes=VMEM_LIMIT_BYTES),
        name="gla",
    )(proj3, proj3, proj3, proj3, code3, w_alpha, b_alpha, gain)


def _out_kernel(a1_ref, a2_ref, a3_ref, s1_ref, s2_ref, s3_ref, za_ref, ob_ref, ga_ref, gb_ref, x_ref,
                expand_ref, wa_ref, wb_ref, wo_ref, bgate_ref, fgain_ref, out_ref, *, chunks):
    tm, d = x_ref.shape
    rc = tm // chunks
    expand = expand_ref[...]
    bg = bgate_ref[...]
    head_lane = lax.broadcasted_iota(jnp.int32, (rc, LANES), 1) < ATTN_HEADS
    ya, yb = [], []
    for c in range(chunks):
        rows = slice(c * rc, (c + 1) * rc)
        yb.append(_dot(ob_ref[rows, :], wb_ref[...]))
        stats = [s_ref[rows, :] for s_ref in (s1_ref, s2_ref, s3_ref)]
        m = jnp.maximum(jnp.maximum(stats[0], stats[1]), stats[2])
        scale = [jnp.exp2(s - m) for s in stats]
        den = [pltpu.roll(s, LANES - DEN_LANE, 1) for s in stats]
        inv = 1.0 / (scale[0] * den[0] + scale[1] * den[1] + scale[2] * den[2])
        oa = None
        for p, a_ref in enumerate((a1_ref, a2_ref, a3_ref)):
            weight = jnp.where(head_lane, scale[p] * inv, 0.0).astype(BF16)
            term = _dot(weight, expand) * a_ref[rows, :].astype(F32)
            oa = term if oa is None else oa + term
        za = za_ref[rows, :].astype(F32)
        ya.append(_dot((oa * (za * _sigmoid(za))).astype(BF16), wa_ref[...]))
    h = []
    for c in range(chunks):
        rows = slice(c * rc, (c + 1) * rc)
        gate_a = _sigmoid(ga_ref[rows, :].astype(F32) + bg[:, :d])
        gate_b = _sigmoid(gb_ref[rows, :].astype(F32) + bg[:, d:])
        merged = (gate_a * ya[c] + gate_b * yb[c]).astype(BF16)
        h.append(x_ref[rows, :] + _dot(merged, wo_ref[...]))
    for c in range(chunks):
        rows = slice(c * rc, (c + 1) * rc)
        ms = jnp.mean(h[c] * h[c], axis=-1, keepdims=True)
        out_ref[rows, :] = h[c] * lax.rsqrt(ms + NORM_EPS) * fgain_ref[...]


def _out_stage(a_pats, s_pats, proj2, ob2, x2, expand, wa, wb, wo, bgate, fgain, *,
               tm, chunks, za_col, ga_col, gb_col):
    t, d = x2.shape
    row_blk = lambda cb: pl.BlockSpec((tm, d), lambda i, cb=cb: (i, cb))
    stat_blk = pl.BlockSpec((tm, LANES), lambda i: (i, 0))
    full = lambda a: pl.BlockSpec(a.shape, lambda i: (0, 0))
    return pl.pallas_call(
        functools.partial(_out_kernel, chunks=chunks),
        grid=(t // tm,),
        in_specs=[row_blk(0), row_blk(0), row_blk(0), stat_blk, stat_blk, stat_blk,
                  row_blk(za_col // d), row_blk(0), row_blk(ga_col // d), row_blk(gb_col // d), row_blk(0),
                  full(expand), full(wa), full(wb), full(wo), full(bgate), full(fgain)],
        out_specs=row_blk(0),
        out_shape=jax.ShapeDtypeStruct((t, d), F32),
        compiler_params=pltpu.CompilerParams(
            dimension_semantics=("parallel",),
            vmem_limit_bytes=VMEM_LIMIT_BYTES),
        name="out_stage",
    )(*a_pats, *s_pats, proj2, ob2, proj2, proj2, x2, expand, wa, wb, wo, bgate, fgain)


def kernel(x, norm_gain, w_in, b_gate, w_alpha, b_alpha, gla_norm_gain,
           w_out_attn, w_out_gla, w_out, final_norm_gain):
    bsz, seq, d = x.shape
    assert w_in.shape[0] == 1, "single-layer block"
    aw = ATTN_HEADS * ATTN_HEAD_DIM
    gk = w_alpha.shape[-1]
    gv = gla_norm_gain.shape[-1]
    sizes = (aw, aw, aw, aw, gk, gk, gv, gv, GLA_GATE_RANK, d, d)
    offs = [0]
    for s in sizes:
        offs.append(offs[-1] + s)
    assert offs[-1] == w_in.shape[-1]
    code_lo, code_hi = offs[8], offs[9]

    w = w_in[0]
    w_front = w[:, :code_lo].astype(BF16)
    w_tail = w[:, code_hi:].astype(BF16)
    w_code = jnp.pad(w[:, code_lo:code_hi], ((0, 0), (0, LANES - GLA_GATE_RANK))).astype(BF16)
    w_alpha_p = jnp.pad(w_alpha[0], ((0, LANES - GLA_GATE_RANK), (0, 0)))
    slopes = 2.0 ** (-8.0 * (jnp.arange(ATTN_HEADS, dtype=F32) + 1.0) / ATTN_HEADS)
    head_of_col = jnp.arange(aw, dtype=jnp.int32) // ATTN_HEAD_DIM
    expand = (jnp.arange(LANES, dtype=jnp.int32)[:, None] == head_of_col[None, :]).astype(BF16)

    x2 = x.reshape(bsz * seq, d)
    proj, code = _in_proj(x2, norm_gain, w_front, w_tail, w_code,
                          first_tile_scale=ATTN_HEAD_DIM ** -0.5 * LOG2E, tm=2048, tn=aw)
    proj3 = proj.reshape(bsz, seq, -1)
    rel = lambda o: o - (GLA_GATE_RANK if o > code_lo else 0)

    pats = [_attention_pattern(proj3, slopes, dilation=dil) for dil in DILATIONS]
    a_pats = [p[0] for p in pats]
    s_pats = [p[1] for p in pats]

    ob = _gla(proj3, code.reshape(bsz, seq, LANES), w_alpha_p, b_alpha, gla_norm_gain,
              q_col=rel(offs[4]), k_col=rel(offs[5]), v_col=rel(offs[6]), z_col=rel(offs[7]), group=16)

    out = _out_stage(a_pats, s_pats, proj, ob.reshape(bsz * seq, gv), x2, expand,
                     w_out_attn[0].astype(BF16), w_out_gla[0].astype(BF16), w_out[0].astype(BF16),
                     b_gate, final_norm_gain.reshape(1, d),
                     tm=512, chunks=2, za_col=rel(offs[3]), ga_col=rel(offs[9]), gb_col=rel(offs[10]))
    return out.reshape(bsz, seq, d)
```

```python
import functools
import math

import jax
import jax.numpy as jnp
from jax import lax
from jax.experimental import pallas as pl
from jax.experimental.pallas import tpu as pltpu

ATTN_HEADS = 16
ATTN_HEAD_DIM = 64
ATTN_WINDOW_STEPS = 128
DILATIONS = (1, 4, 16)
GLA_HEADS = 4
GLA_GATE_RANK = 16
GLA_GATE_TAU = 16.0
GLA_CHUNK = 64
NORM_EPS = 1e-6
GROUP_NORM_EPS = 1e-5

LANES = 128
VMEM_LIMIT_BYTES = 56 * 1024 * 1024

MASK_VALUE = -1e30
LOG2E = math.log2(math.e)
DEN_LANE = ATTN_HEADS
MOVE_ROWS = 512

F32 = jnp.float32
BF16 = jnp.bfloat16


def _dot(a, b):
    return jnp.dot(a, b, preferred_element_type=F32)


def _dot_nt(a, b):
    return lax.dot_general(a, b, (((1,), (1,)), ((), ())), preferred_element_type=F32)


def _dot_tn(a, b):
    return lax.dot_general(a, b, (((0,), (0,)), ((), ())), preferred_element_type=F32)


def _sigmoid(x):
    return 1.0 / (1.0 + jnp.exp(-x))


def _in_proj_kernel(x_ref, gain_ref, wf_ref, wt_ref, wcode_ref, out_ref, code_ref, u_ref, *,
                    n_front, first_tile_scale):
    j = pl.program_id(1)

    @pl.when(j == 0)
    def _():
        x = x_ref[...]
        ms = jnp.mean(x * x, axis=-1, keepdims=True)
        u = (x * lax.rsqrt(ms + NORM_EPS) * gain_ref[...]).astype(BF16)
        u_ref[...] = u
        code_ref[...] = _dot(u, wcode_ref[...])

    @pl.when(j < n_front)
    def _():
        scale = jnp.where(j == 0, first_tile_scale, 1.0).astype(F32)
        out_ref[...] = (_dot(u_ref[...], wf_ref[...]) * scale).astype(out_ref.dtype)

    @pl.when(j >= n_front)
    def _():
        out_ref[...] = _dot(u_ref[...], wt_ref[...]).astype(out_ref.dtype)


def _in_proj(x2, gain, w_front, n_front_cols, w_tail, w_code, *, first_tile_scale, tm, tn):
    t, d = x2.shape
    n_front = n_front_cols // tn
    n = n_front_cols + w_tail.shape[1]
    assert n_front_cols % tn == 0 and w_tail.shape[1] % tn == 0 and n_front_cols <= w_front.shape[1]
    return pl.pallas_call(
        functools.partial(_in_proj_kernel, n_front=n_front, first_tile_scale=first_tile_scale),
        grid=(t // tm, n // tn),
        in_specs=[
            pl.BlockSpec((tm, d), lambda i, j: (i, 0)),
            pl.BlockSpec((1, d), lambda i, j: (0, 0)),
            pl.BlockSpec((d, tn), lambda i, j: (0, jnp.minimum(j, n_front - 1))),
            pl.BlockSpec((d, tn), lambda i, j: (0, jnp.maximum(j - n_front, 0))),
            pl.BlockSpec((d, LANES), lambda i, j: (0, 0)),
        ],
        out_specs=[
            pl.BlockSpec((tm, tn), lambda i, j: (i, j)),
            pl.BlockSpec((tm, LANES), lambda i, j: (i, 0)),
        ],
        out_shape=[
            jax.ShapeDtypeStruct((t, n), BF16),
            jax.ShapeDtypeStruct((t, LANES), F32),
        ],
        scratch_shapes=[pltpu.VMEM((tm, d), BF16)],
        compiler_params=pltpu.CompilerParams(
            dimension_semantics=("parallel", "arbitrary"),
            vmem_limit_bytes=VMEM_LIMIT_BYTES),
        name="in_proj",
    )(x2, gain, w_front, w_tail, w_code)


def _gather_items(src_ref, put, stage_ref, tmp_ref, dilation):
    assert dilation in (4, 16)
    seq = src_ref.shape[1]
    length = seq // dilation
    quarter = seq // 4
    pieces = quarter // MOVE_ROWS
    items = []

    def widen(rows):
        stage_ref[rows, :] = src_ref[0, rows, :].astype(F32)

    for c in range(seq // MOVE_ROWS):
        items.append((0, functools.partial(widen, slice(c * MOVE_ROWS, (c + 1) * MOVE_ROWS))))
    if dilation == 4:
        for r in range(4):
            for c in range(pieces):
                rows = slice(r * quarter + c * MOVE_ROWS, r * quarter + (c + 1) * MOVE_ROWS)
                src = pl.ds(r + 4 * c * MOVE_ROWS, MOVE_ROWS, stride=4)
                items.append((rows.start, lambda rows=rows, src=src: put(rows, stage_ref[src, :])))
        return items

    def to_tmp(rows, src):
        tmp_ref[rows, :] = stage_ref[src, :]

    for r4 in range(4):
        for c in range(pieces):
            rows = slice(r4 * quarter + c * MOVE_ROWS, r4 * quarter + (c + 1) * MOVE_ROWS)
            src = pl.ds(r4 + 4 * c * MOVE_ROWS, MOVE_ROWS, stride=4)
            items.append((r4 * length, functools.partial(to_tmp, rows, src)))
    for r4 in range(4):
        for q4 in range(4):
            r16 = r4 + 4 * q4
            rows = slice(r16 * length, (r16 + 1) * length)
            src = pl.ds(r4 * quarter + q4, length, stride=4)
            items.append((rows.start, lambda rows=rows, src=src: put(rows, tmp_ref[src, :])))
    return items


def _scatter_items(src_ref, tmp_ref, sink, dilation):
    seq = src_ref.shape[0]
    length = seq // dilation
    quarter = seq // 4
    pieces = quarter // MOVE_ROWS
    items = []

    def move(dst_ref, dst, from_ref, rows):
        dst_ref[dst, :] = from_ref[rows, :]

    final_ref = src_ref
    if dilation == 4:
        for r in range(4):
            for c in range(pieces):
                rows = slice(r * quarter + c * MOVE_ROWS, r * quarter + (c + 1) * MOVE_ROWS)
                dst = pl.ds(r + 4 * c * MOVE_ROWS, MOVE_ROWS, stride=4)
                items.append(functools.partial(move, tmp_ref, dst, src_ref, rows))
        final_ref = tmp_ref
    elif dilation == 16:
        for r4 in range(4):
            for q4 in range(4):
                r16 = r4 + 4 * q4
                rows = slice(r16 * length, (r16 + 1) * length)
                dst = pl.ds(r4 * quarter + q4, length, stride=4)
                items.append(functools.partial(move, tmp_ref, dst, src_ref, rows))
        for r4 in range(4):
            for c in range(pieces):
                rows = slice(r4 * quarter + c * MOVE_ROWS, r4 * quarter + (c + 1) * MOVE_ROWS)
                dst = pl.ds(r4 + 4 * c * MOVE_ROWS, MOVE_ROWS, stride=4)
                items.append(functools.partial(move, src_ref, dst, tmp_ref, rows))
    else:
        assert dilation == 1
    for c in range(seq // MOVE_ROWS):
        rows = slice(c * MOVE_ROWS, (c + 1) * MOVE_ROWS)
        items.append(lambda rows=rows: sink(rows, final_ref[rows, :]))
    return items


def _attn_kernel(slopes_ref, qa_ref, ka_ref, va_ref, qb_ref, kb_ref, vb_ref, o_ref, stats_ref,
                 qs_ref, ks_ref, vs_ref, stage_q_ref, stage_k_ref, stage_v_ref, gtmp_q_ref, gtmp_k_ref, gtmp_v_ref,
                 stmp_ref, acc0_ref, acc1_ref, st_ref, *, dilation):
    n = ATTN_WINDOW_STEPS
    seq = qa_ref.shape[1]
    n_blocks = seq // n
    blocks_per_class = seq // dilation // n
    step = pl.program_id(1)
    srcs = ((qa_ref, ka_ref, va_ref), (qb_ref, kb_ref, vb_ref))
    acc_refs = (acc0_ref, acc1_ref)
    stage_refs = (stage_q_ref, stage_k_ref, stage_v_ref)
    gtmp_refs = (gtmp_q_ref, gtmp_k_ref, gtmp_v_ref)

    @pl.when(step == 0)
    def _():
        st_ref[...] = jnp.zeros_like(st_ref)

    def put_q(group):
        def put(rows, vals):
            q = vals.astype(BF16)
            low = lax.broadcasted_iota(jnp.int32, q.shape, 1) < ATTN_HEAD_DIM
            zero = jnp.zeros_like(q)
            qs_ref[group, 0, rows, :] = jnp.where(low, q, zero)
            qs_ref[group, 1, rows, :] = jnp.where(low, zero, q)
        return put

    def put_plain(dst_ref, group):
        def put(rows, vals):
            dst_ref[group, rows, :] = vals.astype(BF16)
        return put

    def gather(group):
        q_src, k_src, v_src = srcs[group]
        if dilation == 1:
            put = put_q(group)
            return [(c * MOVE_ROWS, functools.partial(lambda rows: put(rows, q_src[0, rows, :]),
                                                      slice(c * MOVE_ROWS, (c + 1) * MOVE_ROWS)))
                    for c in range(seq // MOVE_ROWS)]
        items = (_gather_items(q_src, put_q(group), stage_refs[0], gtmp_refs[0], dilation)
                 + _gather_items(k_src, put_plain(ks_ref, group), stage_refs[1], gtmp_refs[1], dilation)
                 + _gather_items(v_src, put_plain(vs_ref, group), stage_refs[2], gtmp_refs[2], dilation))
        return sorted(items, key=lambda pair: pair[0])

    def k_rows(group, rows):
        return srcs[group][1][0, rows, :] if dilation == 1 else ks_ref[group, rows, :]

    def v_rows(group, rows):
        return srcs[group][2][0, rows, :] if dilation == 1 else vs_ref[group, rows, :]

    def scatter(group):
        def sink(rows, vals):
            o_ref[0, rows, group * LANES:(group + 1) * LANES] = vals.astype(o_ref.dtype)
        return _scatter_items(acc_refs[group], stmp_ref, sink, dilation)

    row = lax.broadcasted_iota(jnp.int32, (2 * n, 2 * n), 0)
    col = lax.broadcasted_iota(jnp.int32, (2 * n, 2 * n), 1)
    steps = n + (row % n) - col
    band = (steps >= 0) & (steps <= n)
    dist = (steps * dilation).astype(F32) * LOG2E
    lane = lax.broadcasted_iota(jnp.int32, (n, LANES), 1)
    low_half = lane < ATTN_HEAD_DIM

    def run_group(group, own_items, side_items):
        pair = 2 * step + group
        slope = jnp.where(row < n, slopes_ref[2 * pair], slopes_ref[2 * pair + 1])
        bias = jnp.where(band, -slope * dist, MASK_VALUE)
        bias_first = jnp.where(col < n, MASK_VALUE, bias)
        is_m0, is_m1 = lane == 2 * pair, lane == 2 * pair + 1
        is_d0, is_d1 = lane == DEN_LANE + 2 * pair, lane == DEN_LANE + 2 * pair + 1
        acc_ref = acc_refs[group]

        def block_rows(g):
            return slice(g * n, (g + 1) * n), slice(max(g - 1, 0) * n, max(g, 1) * n)

        def scores(g):
            rows, prev = block_rows(g)
            qq = jnp.concatenate([qs_ref[group, 0, rows, :], qs_ref[group, 1, rows, :]], axis=0)
            kk = jnp.concatenate([k_rows(group, prev), k_rows(group, rows)], axis=0)
            first = g % blocks_per_class == 0
            return _dot_nt(qq, kk) + (bias_first if first else bias)

        def finish(g, s):
            rows, prev = block_rows(g)
            m = jnp.max(s, axis=-1, keepdims=True)
            e = jnp.exp2(s - m)
            den = jnp.sum(e, axis=-1, keepdims=True)
            vv = jnp.concatenate([v_rows(group, prev), v_rows(group, rows)], axis=0)
            pv = _dot(e.astype(BF16), vv)
            acc_ref[rows, :] = jnp.where(low_half, pv[:n], pv[n:])
            cur = st_ref[rows, :]
            cur = jnp.where(is_m0, m[:n], jnp.where(is_m1, m[n:], cur))
            cur = jnp.where(is_d0, den[:n], jnp.where(is_d1, den[n:], cur))
            st_ref[rows, :] = cur

        own = list(own_items)
        side_start = n_blocks // 4 if own else 0

        def issue_own(upto_block):
            while own and own[0][0] < (upto_block + 1) * n:
                own.pop(0)[1]()

        issue_own(1)
        s_next = scores(0)
        for g in range(n_blocks):
            issue_own(4 * (g + 1) + 1)
            s_cur = s_next
            if g + 1 < n_blocks:
                s_next = scores(g + 1)
            finish(g, s_cur)
            if g >= side_start:
                k, span = g - side_start, n_blocks - side_start
                for item in side_items[k * len(side_items) // span:(k + 1) * len(side_items) // span]:
                    item()
        assert not own

    run_group(0, gather(0), [item for _, item in gather(1)])
    run_group(1, [], scatter(0))
    for item in scatter(1):
        item()

    @pl.when(step == pl.num_programs(1) - 1)
    def _():
        def sink(rows, vals):
            stats_ref[0, rows, :] = vals
        for item in _scatter_items(st_ref, stmp_ref, sink, dilation):
            item()


def _attention_pattern(proj3, slopes, *, dilation):
    bsz, seq, _ = proj3.shape
    width = ATTN_HEADS * ATTN_HEAD_DIM
    assert seq % (ATTN_WINDOW_STEPS * dilation) == 0 and seq % (4 * MOVE_ROWS) == 0
    kb = width // LANES

    def in_map(section, group):
        return lambda b, s, *_: (b, 0, section * kb + 2 * s + group)

    kernel = functools.partial(_attn_kernel, dilation=dilation)
    tile = lambda dtype: pltpu.VMEM((seq, LANES), dtype)
    small = pltpu.VMEM((8, LANES), F32)
    o, stats = pl.pallas_call(
        kernel,
        grid_spec=pltpu.PrefetchScalarGridSpec(
            num_scalar_prefetch=1,
            grid=(bsz, kb // 2),
            in_specs=[pl.BlockSpec((1, seq, LANES), in_map(section, group))
                      for group in range(2) for section in range(3)],
            out_specs=[
                pl.BlockSpec((1, seq, 2 * LANES), lambda b, s, *_: (b, 0, s)),
                pl.BlockSpec((1, seq, LANES), lambda b, s, *_: (b, 0, 0)),
            ],
            scratch_shapes=[
                pltpu.VMEM((2, 2, seq, LANES), BF16),
                pltpu.VMEM((2, seq, LANES), BF16),
                pltpu.VMEM((2, seq, LANES), BF16),
                *[tile(F32) if dilation > 1 else small for _ in range(3)],
                *[tile(F32) if dilation == 16 else small for _ in range(3)],
                tile(F32),
                tile(F32),
                tile(F32),
                tile(F32),
            ],
        ),
        out_shape=[
            jax.ShapeDtypeStruct((bsz, seq, width), BF16),
            jax.ShapeDtypeStruct((bsz, seq, LANES), F32),
        ],
        compiler_params=pltpu.CompilerParams(
            dimension_semantics=("parallel", "arbitrary"),
            vmem_limit_bytes=VMEM_LIMIT_BYTES),
        name=f"dilated_attn_d{dilation}",
    )(slopes, *([proj3] * 6))
    return o.reshape(bsz * seq, width), stats.reshape(bsz * seq, LANES)


def _split_bf16(x):
    hi = x.astype(BF16)
    return hi, (x - hi.astype(F32)).astype(BF16)


def _gla_kernel(q_ref, k_ref, v_ref, z_ref, code_ref, walpha_ref, balpha_ref, gain_ref,
                o_ref, b_ref, *, group):
    c = GLA_CHUNK
    seq, dk = q_ref.shape[1], q_ref.shape[2]
    dv = v_ref.shape[2]
    n_groups = seq // (c * group)

    ri = lax.broadcasted_iota(jnp.int32, (c, c), 0)
    ci = lax.broadcasted_iota(jnp.int32, (c, c), 1)
    causal = ri >= ci
    tri = causal.astype(BF16)
    gain = gain_ref[...]
    w_hi, w_lo = _split_bf16(walpha_ref[...])
    b_alpha = balpha_ref[...]

    def chunk_rows(g, u):
        return slice((g * group + u) * c, (g * group + u + 1) * c)

    def gate_logs(g):
        rows = slice(g * group * c, (g + 1) * group * c)
        code_hi, code_lo = _split_bf16(code_ref[0, rows, :])
        logits = _dot(code_hi, w_hi) + _dot(code_hi, w_lo) + _dot(code_lo, w_hi) + b_alpha
        log_sig = jnp.minimum(logits, 0.0) - jnp.log(1.0 + jnp.exp(-jnp.abs(logits)))
        b_ref[rows, :] = log_sig * (LOG2E / GLA_GATE_TAU)

    def cumsums(g):
        for u in range(group):
            rows = chunk_rows(g, u)
            la = b_ref[rows, :]
            la1 = la.astype(BF16)
            rem = la - la1.astype(F32)
            la2 = rem.astype(BF16)
            la3 = (rem - la2.astype(F32)).astype(BF16)
            b_ref[rows, :] = _dot(tri, la1) + _dot(tri, la2) + _dot(tri, la3)

    gate_logs(0)
    cumsums(0)
    st = jnp.zeros((dv, dk), F32)
    for g in range(n_groups):
        rows = [chunk_rows(g, u) for u in range(group)]
        q_dec, k_inv, k_end, decay, v = [], [], [], [], []
        for u in range(group):
            b = b_ref[rows[u], :]
            b_last = b[c - 1:c, :]
            q = q_ref[0, rows[u], :].astype(F32) * (dk ** -0.5)
            k = k_ref[0, rows[u], :].astype(F32)
            q_dec.append((q * jnp.exp2(b)).astype(BF16))
            k_undecayed = k * jnp.exp2(-b)
            decay.append(jnp.exp2(b_last))
            k_inv.append(k_undecayed.astype(BF16))
            k_end.append((k_undecayed * decay[u]).astype(BF16))
            v.append(v_ref[0, rows[u], :])
        a = [jnp.where(causal, _dot_nt(q_dec[u], k_inv[u]), 0.0).astype(BF16) for u in range(group)]
        d_state = [_dot_tn(v[u], k_end[u]) for u in range(group)]
        o_intra = [_dot(a[u], v[u]) for u in range(group)]
        if g + 1 < n_groups:
            gate_logs(g + 1)
        o = []
        for u in range(group):
            o.append(o_intra[u] + _dot_nt(q_dec[u], st.astype(BF16)))
            st = decay[u] * st + d_state[u]
        if g + 1 < n_groups:
            cumsums(g + 1)
        for u in range(group):
            mu = jnp.mean(o[u], axis=-1, keepdims=True)
            cen = o[u] - mu
            var = jnp.mean(cen * cen, axis=-1, keepdims=True)
            y = cen * lax.rsqrt(var + GROUP_NORM_EPS) * gain
            z = z_ref[0, rows[u], :].astype(F32)
            o_ref[0, rows[u], :] = (y * (z * _sigmoid(z))).astype(o_ref.dtype)


def _gla(proj3, code3, w_alpha, b_alpha, gain, *, q_col, k_col, v_col, z_col, group):
    bsz, seq, _ = proj3.shape
    dk = w_alpha.shape[1] // GLA_HEADS
    dv = gain.shape[1] // GLA_HEADS
    assert seq % (GLA_CHUNK * group) == 0
    kernel = functools.partial(_gla_kernel, group=group)
    return pl.pallas_call(
        kernel,
        grid=(bsz, GLA_HEADS),
        in_specs=[
            pl.BlockSpec((1, seq, dk), lambda b, h: (b, 0, q_col // dk + h)),
            pl.BlockSpec((1, seq, dk), lambda b, h: (b, 0, k_col // dk + h)),
            pl.BlockSpec((1, seq, dv), lambda b, h: (b, 0, v_col // dv + h)),
            pl.BlockSpec((1, seq, dv), lambda b, h: (b, 0, z_col // dv + h)),
            pl.BlockSpec((1, seq, LANES), lambda b, h: (b, 0, 0)),
            pl.BlockSpec((LANES, dk), lambda b, h: (0, h)),
            pl.BlockSpec((1, dk), lambda b, h: (0, h)),
            pl.BlockSpec((1, dv), lambda b, h: (0, h)),
        ],
        out_specs=pl.BlockSpec((1, seq, dv), lambda b, h: (b, 0, h)),
        out_shape=jax.ShapeDtypeStruct((bsz, seq, GLA_HEADS * dv), BF16),
        scratch_shapes=[pltpu.VMEM((seq, dk), F32)],
        compiler_params=pltpu.CompilerParams(
            dimension_semantics=("parallel", "parallel"),
            vmem_limit_bytes=VMEM_LIMIT_BYTES),
        name="gla",
    )(proj3, proj3, proj3, proj3, code3, w_alpha, b_alpha, gain)


def _out_kernel(a1_ref, a2_ref, a3_ref, s1_ref, s2_ref, s3_ref, za_ref, ob_ref, ga_ref, gb_ref, x_ref,
                expand_ref, wa_ref, wb_ref, wo_ref, bgate_ref, fgain_ref, out_ref, *, chunks):
    tm, d = x_ref.shape
    rc = tm // chunks
    expand = expand_ref[...]
    bg = bgate_ref[...]
    head_lane = lax.broadcasted_iota(jnp.int32, (rc, LANES), 1) < ATTN_HEADS
    ya, yb = [], []
    for c in range(chunks):
        rows = slice(c * rc, (c + 1) * rc)
        yb.append(_dot(ob_ref[rows, :], wb_ref[...]))
        stats = [s_ref[rows, :] for s_ref in (s1_ref, s2_ref, s3_ref)]
        m = jnp.maximum(jnp.maximum(stats[0], stats[1]), stats[2])
        scale = [jnp.exp2(s - m) for s in stats]
        den = [pltpu.roll(s, LANES - DEN_LANE, 1) for s in stats]
        inv = 1.0 / (scale[0] * den[0] + scale[1] * den[1] + scale[2] * den[2])
        oa = None
        for p, a_ref in enumerate((a1_ref, a2_ref, a3_ref)):
            weight = jnp.where(head_lane, scale[p] * inv, 0.0).astype(BF16)
            term = _dot(weight, expand) * a_ref[rows, :].astype(F32)
            oa = term if oa is None else oa + term
        za = za_ref[rows, :].astype(F32)
        ya.append(_dot((oa * (za * _sigmoid(za))).astype(BF16), wa_ref[...]))
    h = []
    for c in range(chunks):
        rows = slice(c * rc, (c + 1) * rc)
        gate_a = _sigmoid(ga_ref[rows, :].astype(F32) + bg[:, :d])
        gate_b = _sigmoid(gb_ref[rows, :].astype(F32) + bg[:, d:])
        merged = (gate_a * ya[c] + gate_b * yb[c]).astype(BF16)
        h.append(x_ref[rows, :] + _dot(merged, wo_ref[...]))
    for c in range(chunks):
        rows = slice(c * rc, (c + 1) * rc)
        ms = jnp.mean(h[c] * h[c], axis=-1, keepdims=True)
        out_ref[rows, :] = h[c] * lax.rsqrt(ms + NORM_EPS) * fgain_ref[...]


def _out_stage(a_pats, s_pats, proj2, ob2, x2, expand, wa, wb, wo, bgate, fgain, *,
               tm, chunks, za_col, ga_col, gb_col):
    t, d = x2.shape
    row_blk = lambda cb: pl.BlockSpec((tm, d), lambda i, cb=cb: (i, cb))
    stat_blk = pl.BlockSpec((tm, LANES), lambda i: (i, 0))
    full = lambda a: pl.BlockSpec(a.shape, lambda i: (0, 0))
    return pl.pallas_call(
        functools.partial(_out_kernel, chunks=chunks),
        grid=(t // tm,),
        in_specs=[row_blk(0), row_blk(0), row_blk(0), stat_blk, stat_blk, stat_blk,
                  row_blk(za_col // d), row_blk(0), row_blk(ga_col // d), row_blk(gb_col // d), row_blk(0),
                  full(expand), full(wa), full(wb), full(wo), full(bgate), full(fgain)],
        out_specs=row_blk(0),
        out_shape=jax.ShapeDtypeStruct((t, d), F32),
        compiler_params=pltpu.CompilerParams(
            dimension_semantics=("parallel",),
            vmem_limit_bytes=VMEM_LIMIT_BYTES),
        name="out_stage",
    )(*a_pats, *s_pats, proj2, ob2, proj2, proj2, x2, expand, wa, wb, wo, bgate, fgain)


def kernel(x, norm_gain, w_in, b_gate, w_alpha, b_alpha, gla_norm_gain,
           w_out_attn, w_out_gla, w_out, final_norm_gain):
    bsz, seq, d = x.shape
    assert w_in.shape[0] == 1, "single-layer block"
    aw = ATTN_HEADS * ATTN_HEAD_DIM
    gk = w_alpha.shape[-1]
    gv = gla_norm_gain.shape[-1]
    sizes = (aw, aw, aw, aw, gk, gk, gv, gv, GLA_GATE_RANK, d, d)
    offs = [0]
    for s in sizes:
        offs.append(offs[-1] + s)
    assert offs[-1] == w_in.shape[-1]
    code_lo, code_hi = offs[8], offs[9]

    w = w_in[0].astype(BF16)
    w_tail = w[:, code_hi:]
    w_code = jnp.pad(w[:, code_lo:code_hi], ((0, 0), (0, LANES - GLA_GATE_RANK)))
    w_alpha_p = jnp.pad(w_alpha[0], ((0, LANES - GLA_GATE_RANK), (0, 0)))
    slopes = 2.0 ** (-8.0 * (jnp.arange(ATTN_HEADS, dtype=F32) + 1.0) / ATTN_HEADS)
    head_of_col = jnp.arange(aw, dtype=jnp.int32) // ATTN_HEAD_DIM
    expand = (jnp.arange(LANES, dtype=jnp.int32)[:, None] == head_of_col[None, :]).astype(BF16)

    x2 = x.reshape(bsz * seq, d)
    proj, code = _in_proj(x2, norm_gain, w, code_lo, w_tail, w_code,
                          first_tile_scale=ATTN_HEAD_DIM ** -0.5 * LOG2E, tm=2048, tn=aw)
    proj3 = proj.reshape(bsz, seq, -1)
    rel = lambda o: o - (GLA_GATE_RANK if o > code_lo else 0)

    pats = [_attention_pattern(proj3, slopes, dilation=dil) for dil in DILATIONS]
    a_pats = [p[0] for p in pats]
    s_pats = [p[1] for p in pats]

    ob = _gla(proj3, code.reshape(bsz, seq, LANES), w_alpha_p, b_alpha, gla_norm_gain,
              q_col=rel(offs[4]), k_col=rel(offs[5]), v_col=rel(offs[6]), z_col=rel(offs[7]), group=16)

    out = _out_stage(a_pats, s_pats, proj, ob.reshape(bsz * seq, gv), x2, expand,
                     w_out_attn[0].astype(BF16), w_out_gla[0].astype(BF16), w_out[0].astype(BF16),
                     b_gate, final_norm_gain.reshape(1, d),
                     tm=512, chunks=2, za_col=rel(offs[3]), ga_col=rel(offs[9]), gb_col=rel(offs[10]))
    return out.reshape(bsz, seq, d)
```

```python
import functools
import math

import jax
import jax.numpy as jnp
from jax import lax
from jax.experimental import pallas as pl
from jax.experimental.pallas import tpu as pltpu

ATTN_HEADS = 16
ATTN_HEAD_DIM = 64
ATTN_WINDOW_STEPS = 128
DILATIONS = (1, 4, 16)
GLA_HEADS = 4
GLA_GATE_RANK = 16
GLA_GATE_TAU = 16.0
GLA_CHUNK = 64
NORM_EPS = 1e-6
GROUP_NORM_EPS = 1e-5

LANES = 128
VMEM_LIMIT_BYTES = 56 * 1024 * 1024

MASK_VALUE = -1e30
LOG2E = math.log2(math.e)
DEN_LANE = ATTN_HEADS
MOVE_ROWS = 512

IN_PROJ_ROWS = 2048
OUT_ROWS = 512
OUT_CHUNKS = 2
GLA_GROUP = 16

F32 = jnp.float32
BF16 = jnp.bfloat16


def _dot(a, b):
    return jnp.dot(a, b, preferred_element_type=F32)


def _dot_nt(a, b):
    return lax.dot_general(a, b, (((1,), (1,)), ((), ())), preferred_element_type=F32)


def _dot_tn(a, b):
    return lax.dot_general(a, b, (((0,), (0,)), ((), ())), preferred_element_type=F32)


def _sigmoid(x):
    return 1.0 / (1.0 + jnp.exp(-x))


def _in_proj_kernel(x_ref, gain_ref, wf_ref, wt_ref, wcode_ref, out_ref, code_ref, u_ref, *,
                    n_front, first_tile_scale):
    j = pl.program_id(1)

    @pl.when(j == 0)
    def _():
        x = x_ref[...]
        ms = jnp.mean(x * x, axis=-1, keepdims=True)
        u = (x * lax.rsqrt(ms + NORM_EPS) * gain_ref[...]).astype(BF16)
        u_ref[...] = u
        code_ref[...] = _dot(u, wcode_ref[...])

    @pl.when(j < n_front)
    def _():
        scale = jnp.where(j == 0, first_tile_scale, 1.0).astype(F32)
        out_ref[...] = (_dot(u_ref[...], wf_ref[...]) * scale).astype(out_ref.dtype)

    @pl.when(j >= n_front)
    def _():
        out_ref[...] = _dot(u_ref[...], wt_ref[...]).astype(out_ref.dtype)


def _in_proj(x2, gain, w_front, n_front_cols, w_tail, w_code, *, first_tile_scale, tm, tn):
    t, d = x2.shape
    n_front = n_front_cols // tn
    n = n_front_cols + w_tail.shape[1]
    assert n_front_cols % tn == 0 and w_tail.shape[1] % tn == 0 and n_front_cols <= w_front.shape[1]
    return pl.pallas_call(
        functools.partial(_in_proj_kernel, n_front=n_front, first_tile_scale=first_tile_scale),
        grid=(t // tm, n // tn),
        in_specs=[
            pl.BlockSpec((tm, d), lambda i, j: (i, 0)),
            pl.BlockSpec((1, d), lambda i, j: (0, 0)),
            pl.BlockSpec((d, tn), lambda i, j: (0, jnp.minimum(j, n_front - 1))),
            pl.BlockSpec((d, tn), lambda i, j: (0, jnp.maximum(j - n_front, 0))),
            pl.BlockSpec((d, LANES), lambda i, j: (0, 0)),
        ],
        out_specs=[
            pl.BlockSpec((tm, tn), lambda i, j: (i, j)),
            pl.BlockSpec((tm, LANES), lambda i, j: (i, 0)),
        ],
        out_shape=[
            jax.ShapeDtypeStruct((t, n), BF16),
            jax.ShapeDtypeStruct((t, LANES), F32),
        ],
        scratch_shapes=[pltpu.VMEM((tm, d), BF16)],
        compiler_params=pltpu.CompilerParams(
            dimension_semantics=("parallel", "arbitrary"),
            vmem_limit_bytes=VMEM_LIMIT_BYTES),
        name="in_proj",
    )(x2, gain, w_front, w_tail, w_code)


def _gather_items(src_ref, put, stage_ref, tmp_ref, dilation):
    assert dilation in (4, 16)
    seq = src_ref.shape[1]
    length = seq // dilation
    quarter = seq // 4
    pieces = quarter // MOVE_ROWS
    items = []

    def widen(rows):
        stage_ref[rows, :] = src_ref[0, rows, :].astype(F32)

    for c in range(seq // MOVE_ROWS):
        items.append((0, functools.partial(widen, slice(c * MOVE_ROWS, (c + 1) * MOVE_ROWS))))
    if dilation == 4:
        for r in range(4):
            for c in range(pieces):
                rows = slice(r * quarter + c * MOVE_ROWS, r * quarter + (c + 1) * MOVE_ROWS)
                src = pl.ds(r + 4 * c * MOVE_ROWS, MOVE_ROWS, stride=4)
                items.append((rows.start, lambda rows=rows, src=src: put(rows, stage_ref[src, :])))
        return items

    def to_tmp(rows, src):
        tmp_ref[rows, :] = stage_ref[src, :]

    for r4 in range(4):
        for c in range(pieces):
            rows = slice(r4 * quarter + c * MOVE_ROWS, r4 * quarter + (c + 1) * MOVE_ROWS)
            src = pl.ds(r4 + 4 * c * MOVE_ROWS, MOVE_ROWS, stride=4)
            items.append((r4 * length, functools.partial(to_tmp, rows, src)))
    for r4 in range(4):
        for q4 in range(4):
            r16 = r4 + 4 * q4
            rows = slice(r16 * length, (r16 + 1) * length)
            src = pl.ds(r4 * quarter + q4, length, stride=4)
            items.append((rows.start, lambda rows=rows, src=src: put(rows, tmp_ref[src, :])))
    return items


def _scatter_items(src_ref, tmp_ref, sink, dilation):
    seq = src_ref.shape[0]
    length = seq // dilation
    quarter = seq // 4
    pieces = quarter // MOVE_ROWS
    items = []

    def move(dst_ref, dst, from_ref, rows):
        dst_ref[dst, :] = from_ref[rows, :]

    final_ref = src_ref
    if dilation == 4:
        for r in range(4):
            for c in range(pieces):
                rows = slice(r * quarter + c * MOVE_ROWS, r * quarter + (c + 1) * MOVE_ROWS)
                dst = pl.ds(r + 4 * c * MOVE_ROWS, MOVE_ROWS, stride=4)
                items.append(functools.partial(move, tmp_ref, dst, src_ref, rows))
        final_ref = tmp_ref
    elif dilation == 16:
        for r4 in range(4):
            for q4 in range(4):
                r16 = r4 + 4 * q4
                rows = slice(r16 * length, (r16 + 1) * length)
                dst = pl.ds(r4 * quarter + q4, length, stride=4)
                items.append(functools.partial(move, tmp_ref, dst, src_ref, rows))
        for r4 in range(4):
            for c in range(pieces):
                rows = slice(r4 * quarter + c * MOVE_ROWS, r4 * quarter + (c + 1) * MOVE_ROWS)
                dst = pl.ds(r4 + 4 * c * MOVE_ROWS, MOVE_ROWS, stride=4)
                items.append(functools.partial(move, src_ref, dst, tmp_ref, rows))
    else:
        assert dilation == 1
    for c in range(seq // MOVE_ROWS):
        rows = slice(c * MOVE_ROWS, (c + 1) * MOVE_ROWS)
        items.append(lambda rows=rows: sink(rows, final_ref[rows, :]))
    return items


def _attn_kernel(slopes_ref, qa_ref, ka_ref, va_ref, qb_ref, kb_ref, vb_ref, o_ref, stats_ref,
                 qs_ref, ks_ref, vs_ref, stage_q_ref, stage_k_ref, stage_v_ref, gtmp_q_ref, gtmp_k_ref, gtmp_v_ref,
                 stmp_ref, acc0_ref, acc1_ref, st_ref, *, dilation):
    n = ATTN_WINDOW_STEPS
    seq = qa_ref.shape[1]
    n_blocks = seq // n
    blocks_per_class = seq // dilation // n
    step = pl.program_id(1)
    srcs = ((qa_ref, ka_ref, va_ref), (qb_ref, kb_ref, vb_ref))
    acc_refs = (acc0_ref, acc1_ref)
    stage_refs = (stage_q_ref, stage_k_ref, stage_v_ref)
    gtmp_refs = (gtmp_q_ref, gtmp_k_ref, gtmp_v_ref)

    @pl.when(step == 0)
    def _():
        st_ref[...] = jnp.zeros_like(st_ref)

    def put_q(group):
        def put(rows, vals):
            q = vals.astype(BF16)
            low = lax.broadcasted_iota(jnp.int32, q.shape, 1) < ATTN_HEAD_DIM
            zero = jnp.zeros_like(q)
            qs_ref[group, 0, rows, :] = jnp.where(low, q, zero)
            qs_ref[group, 1, rows, :] = jnp.where(low, zero, q)
        return put

    def put_plain(dst_ref, group):
        def put(rows, vals):
            dst_ref[group, rows, :] = vals.astype(BF16)
        return put

    def gather(group):
        q_src, k_src, v_src = srcs[group]
        if dilation == 1:
            put = put_q(group)
            return [(c * MOVE_ROWS, functools.partial(lambda rows: put(rows, q_src[0, rows, :]),
                                                      slice(c * MOVE_ROWS, (c + 1) * MOVE_ROWS)))
                    for c in range(seq // MOVE_ROWS)]
        items = (_gather_items(q_src, put_q(group), stage_refs[0], gtmp_refs[0], dilation)
                 + _gather_items(k_src, put_plain(ks_ref, group), stage_refs[1], gtmp_refs[1], dilation)
                 + _gather_items(v_src, put_plain(vs_ref, group), stage_refs[2], gtmp_refs[2], dilation))
        return sorted(items, key=lambda pair: pair[0])

    def k_rows(group, rows):
        return srcs[group][1][0, rows, :] if dilation == 1 else ks_ref[group, rows, :]

    def v_rows(group, rows):
        return srcs[group][2][0, rows, :] if dilation == 1 else vs_ref[group, rows, :]

    def scatter(group):
        def sink(rows, vals):
            o_ref[0, rows, group * LANES:(group + 1) * LANES] = vals.astype(o_ref.dtype)
        return _scatter_items(acc_refs[group], stmp_ref, sink, dilation)

    row = lax.broadcasted_iota(jnp.int32, (2 * n, 2 * n), 0)
    col = lax.broadcasted_iota(jnp.int32, (2 * n, 2 * n), 1)
    steps = n + (row % n) - col
    band = (steps >= 0) & (steps <= n)
    dist = (steps * dilation).astype(F32) * LOG2E
    lane = lax.broadcasted_iota(jnp.int32, (n, LANES), 1)
    low_half = lane < ATTN_HEAD_DIM

    def run_group(group, own_items, side_items):
        pair = 2 * step + group
        slope = jnp.where(row < n, slopes_ref[2 * pair], slopes_ref[2 * pair + 1])
        bias = jnp.where(band, -slope * dist, MASK_VALUE)
        bias_first = jnp.where(col < n, MASK_VALUE, bias)
        is_m0, is_m1 = lane == 2 * pair, lane == 2 * pair + 1
        is_d0, is_d1 = lane == DEN_LANE + 2 * pair, lane == DEN_LANE + 2 * pair + 1
        acc_ref = acc_refs[group]

        def block_rows(g):
            return slice(g * n, (g + 1) * n), slice(max(g - 1, 0) * n, max(g, 1) * n)

        def scores(g):
            rows, prev = block_rows(g)
            qq = jnp.concatenate([qs_ref[group, 0, rows, :], qs_ref[group, 1, rows, :]], axis=0)
            kk = jnp.concatenate([k_rows(group, prev), k_rows(group, rows)], axis=0)
            first = g % blocks_per_class == 0
            return _dot_nt(qq, kk) + (bias_first if first else bias)

        def finish(g, s):
            rows, prev = block_rows(g)
            m = jnp.max(s, axis=-1, keepdims=True)
            e = jnp.exp2(s - m)
            den = jnp.sum(e, axis=-1, keepdims=True)
            vv = jnp.concatenate([v_rows(group, prev), v_rows(group, rows)], axis=0)
            pv = _dot(e.astype(BF16), vv)
            acc_ref[rows, :] = jnp.where(low_half, pv[:n], pv[n:])
            cur = st_ref[rows, :]
            cur = jnp.where(is_m0, m[:n], jnp.where(is_m1, m[n:], cur))
            cur = jnp.where(is_d0, den[:n], jnp.where(is_d1, den[n:], cur))
            st_ref[rows, :] = cur

        own = list(own_items)
        side_start = n_blocks // 4 if own else 0

        def issue_own(upto_block):
            while own and own[0][0] < (upto_block + 1) * n:
                own.pop(0)[1]()

        issue_own(1)
        s_next = scores(0)
        for g in range(n_blocks):
            issue_own(4 * (g + 1) + 1)
            s_cur = s_next
            if g + 1 < n_blocks:
                s_next = scores(g + 1)
            finish(g, s_cur)
            if g >= side_start:
                k, span = g - side_start, n_blocks - side_start
                for item in side_items[k * len(side_items) // span:(k + 1) * len(side_items) // span]:
                    item()
        assert not own

    run_group(0, gather(0), [item for _, item in gather(1)])
    run_group(1, [], scatter(0))
    for item in scatter(1):
        item()

    @pl.when(step == pl.num_programs(1) - 1)
    def _():
        def sink(rows, vals):
            stats_ref[0, rows, :] = vals
        for item in _scatter_items(st_ref, stmp_ref, sink, dilation):
            item()


def _attention_pattern(proj3, slopes, *, dilation):
    bsz, seq, _ = proj3.shape
    width = ATTN_HEADS * ATTN_HEAD_DIM
    assert seq % (ATTN_WINDOW_STEPS * dilation) == 0 and seq % (4 * MOVE_ROWS) == 0
    kb = width // LANES

    def in_map(section, group):
        return lambda b, s, *_: (b, 0, section * kb + 2 * s + group)

    kernel = functools.partial(_attn_kernel, dilation=dilation)
    tile = lambda dtype: pltpu.VMEM((seq, LANES), dtype)
    small = pltpu.VMEM((8, LANES), F32)
    o, stats = pl.pallas_call(
        kernel,
        grid_spec=pltpu.PrefetchScalarGridSpec(
            num_scalar_prefetch=1,
            grid=(bsz, kb // 2),
            in_specs=[pl.BlockSpec((1, seq, LANES), in_map(section, group))
                      for group in range(2) for section in range(3)],
            out_specs=[
                pl.BlockSpec((1, seq, 2 * LANES), lambda b, s, *_: (b, 0, s)),
                pl.BlockSpec((1, seq, LANES), lambda b, s, *_: (b, 0, 0)),
            ],
            scratch_shapes=[
                pltpu.VMEM((2, 2, seq, LANES), BF16),
                pltpu.VMEM((2, seq, LANES), BF16),
                pltpu.VMEM((2, seq, LANES), BF16),
                *[tile(F32) if dilation > 1 else small for _ in range(3)],
                *[tile(F32) if dilation == 16 else small for _ in range(3)],
                tile(F32),
                tile(F32),
                tile(F32),
                tile(F32),
            ],
        ),
        out_shape=[
            jax.ShapeDtypeStruct((bsz, seq, width), BF16),
            jax.ShapeDtypeStruct((bsz, seq, LANES), F32),
        ],
        compiler_params=pltpu.CompilerParams(
            dimension_semantics=("parallel", "arbitrary"),
            vmem_limit_bytes=VMEM_LIMIT_BYTES),
        name=f"dilated_attn_d{dilation}",
    )(slopes, *([proj3] * 6))
    return o.reshape(bsz * seq, width), stats.reshape(bsz * seq, LANES)


def _split_bf16(x):
    hi = x.astype(BF16)
    return hi, (x - hi.astype(F32)).astype(BF16)


def _gla_kernel(q_ref, k_ref, v_ref, z_ref, code_ref, walpha_ref, balpha_ref, gain_ref,
                o_ref, b_ref, *, group):
    c = GLA_CHUNK
    seq, dk = q_ref.shape[1], q_ref.shape[2]
    dv = v_ref.shape[2]
    n_groups = seq // (c * group)

    ri = lax.broadcasted_iota(jnp.int32, (c, c), 0)
    ci = lax.broadcasted_iota(jnp.int32, (c, c), 1)
    causal = ri >= ci
    tri = causal.astype(BF16)
    gain = gain_ref[...]
    w_hi, w_lo = _split_bf16(walpha_ref[...])
    b_alpha = balpha_ref[...]

    def chunk_rows(g, u):
        return slice((g * group + u) * c, (g * group + u + 1) * c)

    def gate_logs(g):
        rows = slice(g * group * c, (g + 1) * group * c)
        code_hi, code_lo = _split_bf16(code_ref[0, rows, :])
        logits = _dot(code_hi, w_hi) + _dot(code_hi, w_lo) + _dot(code_lo, w_hi) + b_alpha
        log_sig = jnp.minimum(logits, 0.0) - jnp.log(1.0 + jnp.exp(-jnp.abs(logits)))
        b_ref[rows, :] = log_sig * (LOG2E / GLA_GATE_TAU)

    def cumsums(g):
        for u in range(group):
            rows = chunk_rows(g, u)
            la = b_ref[rows, :]
            la1 = la.astype(BF16)
            rem = la - la1.astype(F32)
            la2 = rem.astype(BF16)
            la3 = (rem - la2.astype(F32)).astype(BF16)
            b_ref[rows, :] = _dot(tri, la1) + _dot(tri, la2) + _dot(tri, la3)

    gate_logs(0)
    cumsums(0)
    st = jnp.zeros((dv, dk), F32)
    for g in range(n_groups):
        rows = [chunk_rows(g, u) for u in range(group)]
        q_dec, k_inv, k_end, decay, v = [], [], [], [], []
        for u in range(group):
            b = b_ref[rows[u], :]
            b_last = b[c - 1:c, :]
            q = q_ref[0, rows[u], :].astype(F32) * (dk ** -0.5)
            k = k_ref[0, rows[u], :].astype(F32)
            q_dec.append((q * jnp.exp2(b)).astype(BF16))
            k_undecayed = k * jnp.exp2(-b)
            decay.append(jnp.exp2(b_last))
            k_inv.append(k_undecayed.astype(BF16))
            k_end.append((k_undecayed * decay[u]).astype(BF16))
            v.append(v_ref[0, rows[u], :])
        a = [jnp.where(causal, _dot_nt(q_dec[u], k_inv[u]), 0.0).astype(BF16) for u in range(group)]
        d_state = [_dot_tn(v[u], k_end[u]) for u in range(group)]
        o_intra = [_dot(a[u], v[u]) for u in range(group)]
        if g + 1 < n_groups:
            gate_logs(g + 1)
        o = []
        for u in range(group):
            o.append(o_intra[u] + _dot_nt(q_dec[u], st.astype(BF16)))
            st = decay[u] * st + d_state[u]
        if g + 1 < n_groups:
            cumsums(g + 1)
        for u in range(group):
            mu = jnp.mean(o[u], axis=-1, keepdims=True)
            cen = o[u] - mu
            var = jnp.mean(cen * cen, axis=-1, keepdims=True)
            y = cen * lax.rsqrt(var + GROUP_NORM_EPS) * gain
            z = z_ref[0, rows[u], :].astype(F32)
            o_ref[0, rows[u], :] = (y * (z * _sigmoid(z))).astype(o_ref.dtype)


def _gla(proj3, code3, w_alpha, b_alpha, gain, *, q_col, k_col, v_col, z_col, group):
    bsz, seq, _ = proj3.shape
    dk = w_alpha.shape[1] // GLA_HEADS
    dv = gain.shape[1] // GLA_HEADS
    assert seq % (GLA_CHUNK * group) == 0
    kernel = functools.partial(_gla_kernel, group=group)
    return pl.pallas_call(
        kernel,
        grid=(bsz, GLA_HEADS),
        in_specs=[
            pl.BlockSpec((1, seq, dk), lambda b, h: (b, 0, q_col // dk + h)),
            pl.BlockSpec((1, seq, dk), lambda b, h: (b, 0, k_col // dk + h)),
            pl.BlockSpec((1, seq, dv), lambda b, h: (b, 0, v_col // dv + h)),
            pl.BlockSpec((1, seq, dv), lambda b, h: (b, 0, z_col // dv + h)),
            pl.BlockSpec((1, seq, LANES), lambda b, h: (b, 0, 0)),
            pl.BlockSpec((LANES, dk), lambda b, h: (0, h)),
            pl.BlockSpec((1, dk), lambda b, h: (0, h)),
            pl.BlockSpec((1, dv), lambda b, h: (0, h)),
        ],
        out_specs=pl.BlockSpec((1, seq, dv), lambda b, h: (b, 0, h)),
        out_shape=jax.ShapeDtypeStruct((bsz, seq, GLA_HEADS * dv), BF16),
        scratch_shapes=[pltpu.VMEM((seq, dk), F32)],
        compiler_params=pltpu.CompilerParams(
            dimension_semantics=("parallel", "parallel"),
            vmem_limit_bytes=VMEM_LIMIT_BYTES),
        name="gla",
    )(proj3, proj3, proj3, proj3, code3, w_alpha, b_alpha, gain)


def _out_kernel(a1_ref, a2_ref, a3_ref, s1_ref, s2_ref, s3_ref, za_ref, ob_ref, ga_ref, gb_ref, x_ref,
                expand_ref, wa_ref, wb_ref, wo_ref, bgate_ref, fgain_ref, out_ref, *, chunks):
    tm, d = x_ref.shape
    rc = tm // chunks
    expand = expand_ref[...]
    bg = bgate_ref[...]
    head_lane = lax.broadcasted_iota(jnp.int32, (rc, LANES), 1) < ATTN_HEADS
    ya, yb = [], []
    for c in range(chunks):
        rows = slice(c * rc, (c + 1) * rc)
        yb.append(_dot(ob_ref[rows, :], wb_ref[...]))
        stats = [s_ref[rows, :] for s_ref in (s1_ref, s2_ref, s3_ref)]
        m = jnp.maximum(jnp.maximum(stats[0], stats[1]), stats[2])
        scale = [jnp.exp2(s - m) for s in stats]
        den = [pltpu.roll(s, LANES - DEN_LANE, 1) for s in stats]
        inv = 1.0 / (scale[0] * den[0] + scale[1] * den[1] + scale[2] * den[2])
        oa = None
        for p, a_ref in enumerate((a1_ref, a2_ref, a3_ref)):
            weight = jnp.where(head_lane, scale[p] * inv, 0.0).astype(BF16)
            term = _dot(weight, expand) * a_ref[rows, :].astype(F32)
            oa = term if oa is None else oa + term
        za = za_ref[rows, :].astype(F32)
        ya.append(_dot((oa * (za * _sigmoid(za))).astype(BF16), wa_ref[...]))
    h = []
    for c in range(chunks):
        rows = slice(c * rc, (c + 1) * rc)
        gate_a = _sigmoid(ga_ref[rows, :].astype(F32) + bg[:, :d])
        gate_b = _sigmoid(gb_ref[rows, :].astype(F32) + bg[:, d:])
        merged = (gate_a * ya[c] + gate_b * yb[c]).astype(BF16)
        h.append(x_ref[rows, :] + _dot(merged, wo_ref[...]))
    for c in range(chunks):
        rows = slice(c * rc, (c + 1) * rc)
        ms = jnp.mean(h[c] * h[c], axis=-1, keepdims=True)
        out_ref[rows, :] = h[c] * lax.rsqrt(ms + NORM_EPS) * fgain_ref[...]


def _out_stage(a_pats, s_pats, proj2, ob2, x2, expand, wa, wb, wo, bgate, fgain, *,
               tm, chunks, za_col, ga_col, gb_col):
    t, d = x2.shape
    row_blk = lambda cb: pl.BlockSpec((tm, d), lambda i, cb=cb: (i, cb))
    stat_blk = pl.BlockSpec((tm, LANES), lambda i: (i, 0))
    full = lambda a: pl.BlockSpec(a.shape, lambda i: (0, 0))
    return pl.pallas_call(
        functools.partial(_out_kernel, chunks=chunks),
        grid=(t // tm,),
        in_specs=[row_blk(0), row_blk(0), row_blk(0), stat_blk, stat_blk, stat_blk,
                  row_blk(za_col // d), row_blk(0), row_blk(ga_col // d), row_blk(gb_col // d), row_blk(0),
                  full(expand), full(wa), full(wb), full(wo), full(bgate), full(fgain)],
        out_specs=row_blk(0),
        out_shape=jax.ShapeDtypeStruct((t, d), F32),
        compiler_params=pltpu.CompilerParams(
            dimension_semantics=("parallel",),
            vmem_limit_bytes=VMEM_LIMIT_BYTES),
        name="out_stage",
    )(*a_pats, *s_pats, proj2, ob2, proj2, proj2, x2, expand, wa, wb, wo, bgate, fgain)


def kernel(x, norm_gain, w_in, b_gate, w_alpha, b_alpha, gla_norm_gain,
           w_out_attn, w_out_gla, w_out, final_norm_gain):
    bsz, seq, d = x.shape
    assert w_in.shape[0] == 1, "single-layer block"
    aw = ATTN_HEADS * ATTN_HEAD_DIM
    gk = w_alpha.shape[-1]
    gv = gla_norm_gain.shape[-1]
    sizes = (aw, aw, aw, aw, gk, gk, gv, gv, GLA_GATE_RANK, d, d)
    offs = [0]
    for s in sizes:
        offs.append(offs[-1] + s)
    assert offs[-1] == w_in.shape[-1]
    code_lo, code_hi = offs[8], offs[9]

    w = w_in[0].astype(BF16)
    w_tail = w[:, code_hi:]
    w_code = jnp.pad(w[:, code_lo:code_hi], ((0, 0), (0, LANES - GLA_GATE_RANK)))
    w_alpha_p = jnp.pad(w_alpha[0], ((0, LANES - GLA_GATE_RANK), (0, 0)))
    slopes = 2.0 ** (-8.0 * (jnp.arange(ATTN_HEADS, dtype=F32) + 1.0) / ATTN_HEADS)
    head_of_col = jnp.arange(aw, dtype=jnp.int32) // ATTN_HEAD_DIM
    expand = (jnp.arange(LANES, dtype=jnp.int32)[:, None] == head_of_col[None, :]).astype(BF16)

    x2 = x.reshape(bsz * seq, d)
    proj, code = _in_proj(x2, norm_gain, w, code_lo, w_tail, w_code,
                          first_tile_scale=ATTN_HEAD_DIM ** -0.5 * LOG2E, tm=IN_PROJ_ROWS, tn=aw)
    proj3 = proj.reshape(bsz, seq, -1)
    rel = lambda o: o - (GLA_GATE_RANK if o > code_lo else 0)

    pats = [_attention_pattern(proj3, slopes, dilation=dil) for dil in DILATIONS]
    a_pats = [p[0] for p in pats]
    s_pats = [p[1] for p in pats]

    ob = _gla(proj3, code.reshape(bsz, seq, LANES), w_alpha_p, b_alpha, gla_norm_gain,
              q_col=rel(offs[4]), k_col=rel(offs[5]), v_col=rel(offs[6]), z_col=rel(offs[7]), group=GLA_GROUP)

    out = _out_stage(a_pats, s_pats, proj, ob.reshape(bsz * seq, gv), x2, expand,
                     w_out_attn[0].astype(BF16), w_out_gla[0].astype(BF16), w_out[0].astype(BF16),
                     b_gate, final_norm_gain.reshape(1, d),
                     tm=OUT_ROWS, chunks=OUT_CHUNKS, za_col=rel(offs[3]), ga_col=rel(offs[9]), gb_col=rel(offs[10]))
    return out.reshape(bsz, seq, d)
```

```python
import functools
import math

import jax
import jax.numpy as jnp
from jax import lax
from jax.experimental import pallas as pl
from jax.experimental.pallas import tpu as pltpu

ATTN_HEADS = 16
ATTN_HEAD_DIM = 64
ATTN_WINDOW_STEPS = 128
DILATIONS = (1, 4, 16)
GLA_HEADS = 4
GLA_GATE_RANK = 16
GLA_GATE_TAU = 16.0
GLA_CHUNK = 64
NORM_EPS = 1e-6
GROUP_NORM_EPS = 1e-5

LANES = 128
VMEM_LIMIT_BYTES = 56 * 1024 * 1024

MASK_VALUE = -1e30
LOG2E = math.log2(math.e)
DEN_LANE = ATTN_HEADS
MOVE_ROWS = 512

IN_PROJ_ROWS = 2048
OUT_ROWS = 512
OUT_CHUNKS = 2
GLA_GROUP = 16

F32 = jnp.float32
BF16 = jnp.bfloat16


def _dot(a, b):
    return jnp.dot(a, b, preferred_element_type=F32)


def _dot_nt(a, b):
    return lax.dot_general(a, b, (((1,), (1,)), ((), ())), preferred_element_type=F32)


def _dot_tn(a, b):
    return lax.dot_general(a, b, (((0,), (0,)), ((), ())), preferred_element_type=F32)


def _sigmoid(x):
    return 1.0 / (1.0 + jnp.exp(-x))


def _in_proj_kernel(x_ref, gain_ref, wf_ref, wt_ref, wcode_ref, out_ref, code_ref, u_ref, *,
                    n_front, first_tile_scale):
    j = pl.program_id(1)

    @pl.when(j == 0)
    def _():
        x = x_ref[...]
        ms = jnp.mean(x * x, axis=-1, keepdims=True)
        u = (x * lax.rsqrt(ms + NORM_EPS) * gain_ref[...]).astype(BF16)
        u_ref[...] = u
        code_ref[...] = _dot(u, wcode_ref[...])

    w = jnp.where(j < n_front, wf_ref[...], wt_ref[...])
    scale = jnp.where(j == 0, first_tile_scale, 1.0).astype(F32)
    out_ref[...] = (_dot(u_ref[...], w) * scale).astype(out_ref.dtype)


def _in_proj(x2, gain, w_front, n_front_cols, w_tail, w_code, *, first_tile_scale, tm, tn):
    t, d = x2.shape
    n_front = n_front_cols // tn
    n = n_front_cols + w_tail.shape[1]
    assert n_front_cols % tn == 0 and w_tail.shape[1] % tn == 0 and n_front_cols <= w_front.shape[1]
    return pl.pallas_call(
        functools.partial(_in_proj_kernel, n_front=n_front, first_tile_scale=first_tile_scale),
        grid=(t // tm, n // tn),
        in_specs=[
            pl.BlockSpec((tm, d), lambda i, j: (i, 0)),
            pl.BlockSpec((1, d), lambda i, j: (0, 0)),
            pl.BlockSpec((d, tn), lambda i, j: (0, jnp.minimum(j, n_front - 1))),
            pl.BlockSpec((d, tn), lambda i, j: (0, jnp.maximum(j - n_front, 0))),
            pl.BlockSpec((d, LANES), lambda i, j: (0, 0)),
        ],
        out_specs=[
            pl.BlockSpec((tm, tn), lambda i, j: (i, j)),
            pl.BlockSpec((tm, LANES), lambda i, j: (i, 0)),
        ],
        out_shape=[
            jax.ShapeDtypeStruct((t, n), BF16),
            jax.ShapeDtypeStruct((t, LANES), F32),
        ],
        scratch_shapes=[pltpu.VMEM((tm, d), BF16)],
        compiler_params=pltpu.CompilerParams(
            dimension_semantics=("parallel", "arbitrary"),
            vmem_limit_bytes=VMEM_LIMIT_BYTES),
        name="in_proj",
    )(x2, gain, w_front, w_tail, w_code)


def _gather_items(src_ref, put, stage_ref, tmp_ref, dilation):
    assert dilation in (4, 16)
    seq = src_ref.shape[1]
    length = seq // dilation
    quarter = seq // 4
    pieces = quarter // MOVE_ROWS
    items = []

    def widen(rows):
        stage_ref[rows, :] = src_ref[0, rows, :].astype(F32)

    for c in range(seq // MOVE_ROWS):
        items.append((0, functools.partial(widen, slice(c * MOVE_ROWS, (c + 1) * MOVE_ROWS))))
    if dilation == 4:
        for r in range(4):
            for c in range(pieces):
                rows = slice(r * quarter + c * MOVE_ROWS, r * quarter + (c + 1) * MOVE_ROWS)
                src = pl.ds(r + 4 * c * MOVE_ROWS, MOVE_ROWS, stride=4)
                items.append((rows.start, lambda rows=rows, src=src: put(rows, stage_ref[src, :])))
        return items

    def to_tmp(rows, src):
        tmp_ref[rows, :] = stage_ref[src, :]

    for r4 in range(4):
        for c in range(pieces):
            rows = slice(r4 * quarter + c * MOVE_ROWS, r4 * quarter + (c + 1) * MOVE_ROWS)
            src = pl.ds(r4 + 4 * c * MOVE_ROWS, MOVE_ROWS, stride=4)
            items.append((r4 * length, functools.partial(to_tmp, rows, src)))
    for r4 in range(4):
        for q4 in range(4):
            r16 = r4 + 4 * q4
            rows = slice(r16 * length, (r16 + 1) * length)
            src = pl.ds(r4 * quarter + q4, length, stride=4)
            items.append((rows.start, lambda rows=rows, src=src: put(rows, tmp_ref[src, :])))
    return items


def _scatter_items(src_ref, tmp_ref, sink, dilation):
    seq = src_ref.shape[0]
    length = seq // dilation
    quarter = seq // 4
    pieces = quarter // MOVE_ROWS
    items = []

    def move(dst_ref, dst, from_ref, rows):
        dst_ref[dst, :] = from_ref[rows, :]

    final_ref = src_ref
    if dilation == 4:
        for r in range(4):
            for c in range(pieces):
                rows = slice(r * quarter + c * MOVE_ROWS, r * quarter + (c + 1) * MOVE_ROWS)
                dst = pl.ds(r + 4 * c * MOVE_ROWS, MOVE_ROWS, stride=4)
                items.append(functools.partial(move, tmp_ref, dst, src_ref, rows))
        final_ref = tmp_ref
    elif dilation == 16:
        for r4 in range(4):
            for q4 in range(4):
                r16 = r4 + 4 * q4
                rows = slice(r16 * length, (r16 + 1) * length)
                dst = pl.ds(r4 * quarter + q4, length, stride=4)
                items.append(functools.partial(move, tmp_ref, dst, src_ref, rows))
        for r4 in range(4):
            for c in range(pieces):
                rows = slice(r4 * quarter + c * MOVE_ROWS, r4 * quarter + (c + 1) * MOVE_ROWS)
                dst = pl.ds(r4 + 4 * c * MOVE_ROWS, MOVE_ROWS, stride=4)
                items.append(functools.partial(move, src_ref, dst, tmp_ref, rows))
    else:
        assert dilation == 1
    for c in range(seq // MOVE_ROWS):
        rows = slice(c * MOVE_ROWS, (c + 1) * MOVE_ROWS)
        items.append(lambda rows=rows: sink(rows, final_ref[rows, :]))
    return items


def _attn_kernel(slopes_ref, qa_ref, ka_ref, va_ref, qb_ref, kb_ref, vb_ref, o_ref, stats_ref,
                 qs_ref, ks_ref, vs_ref, stage_q_ref, stage_k_ref, stage_v_ref, gtmp_q_ref, gtmp_k_ref, gtmp_v_ref,
                 stmp_ref, acc0_ref, acc1_ref, st_ref, *, dilation):
    n = ATTN_WINDOW_STEPS
    seq = qa_ref.shape[1]
    n_blocks = seq // n
    blocks_per_class = seq // dilation // n
    step = pl.program_id(1)
    srcs = ((qa_ref, ka_ref, va_ref), (qb_ref, kb_ref, vb_ref))
    acc_refs = (acc0_ref, acc1_ref)
    stage_refs = (stage_q_ref, stage_k_ref, stage_v_ref)
    gtmp_refs = (gtmp_q_ref, gtmp_k_ref, gtmp_v_ref)

    @pl.when(step == 0)
    def _():
        st_ref[...] = jnp.zeros_like(st_ref)

    def put_q(group):
        def put(rows, vals):
            q = vals.astype(BF16)
            low = lax.broadcasted_iota(jnp.int32, q.shape, 1) < ATTN_HEAD_DIM
            zero = jnp.zeros_like(q)
            qs_ref[group, 0, rows, :] = jnp.where(low, q, zero)
            qs_ref[group, 1, rows, :] = jnp.where(low, zero, q)
        return put

    def put_plain(dst_ref, group):
        def put(rows, vals):
            dst_ref[group, rows, :] = vals.astype(BF16)
        return put

    def gather(group):
        q_src, k_src, v_src = srcs[group]
        if dilation == 1:
            put = put_q(group)
            return [(c * MOVE_ROWS, functools.partial(lambda rows: put(rows, q_src[0, rows, :]),
                                                      slice(c * MOVE_ROWS, (c + 1) * MOVE_ROWS)))
                    for c in range(seq // MOVE_ROWS)]
        items = (_gather_items(q_src, put_q(group), stage_refs[0], gtmp_refs[0], dilation)
                 + _gather_items(k_src, put_plain(ks_ref, group), stage_refs[1], gtmp_refs[1], dilation)
                 + _gather_items(v_src, put_plain(vs_ref, group), stage_refs[2], gtmp_refs[2], dilation))
        return sorted(items, key=lambda pair: pair[0])

    def k_rows(group, rows):
        return srcs[group][1][0, rows, :] if dilation == 1 else ks_ref[group, rows, :]

    def v_rows(group, rows):
        return srcs[group][2][0, rows, :] if dilation == 1 else vs_ref[group, rows, :]

    def scatter(group):
        def sink(rows, vals):
            o_ref[0, rows, group * LANES:(group + 1) * LANES] = vals.astype(o_ref.dtype)
        return _scatter_items(acc_refs[group], stmp_ref, sink, dilation)

    row = lax.broadcasted_iota(jnp.int32, (2 * n, 2 * n), 0)
    col = lax.broadcasted_iota(jnp.int32, (2 * n, 2 * n), 1)
    steps = n + (row % n) - col
    band = (steps >= 0) & (steps <= n)
    dist = (steps * dilation).astype(F32) * LOG2E
    lane = lax.broadcasted_iota(jnp.int32, (n, LANES), 1)
    low_half = lane < ATTN_HEAD_DIM

    def run_group(group, own_items, side_items):
        pair = 2 * step + group
        slope = jnp.where(row < n, slopes_ref[2 * pair], slopes_ref[2 * pair + 1])
        bias = jnp.where(band, -slope * dist, MASK_VALUE)
        bias_first = jnp.where(col < n, MASK_VALUE, bias)
        is_m0, is_m1 = lane == 2 * pair, lane == 2 * pair + 1
        is_d0, is_d1 = lane == DEN_LANE + 2 * pair, lane == DEN_LANE + 2 * pair + 1
        acc_ref = acc_refs[group]

        def block_rows(g):
            return slice(g * n, (g + 1) * n), slice(max(g - 1, 0) * n, max(g, 1) * n)

        def scores(g):
            rows, prev = block_rows(g)
            qq = jnp.concatenate([qs_ref[group, 0, rows, :], qs_ref[group, 1, rows, :]], axis=0)
            kk = jnp.concatenate([k_rows(group, prev), k_rows(group, rows)], axis=0)
            first = g % blocks_per_class == 0
            return _dot_nt(qq, kk) + (bias_first if first else bias)

        def finish(g, s):
            rows, prev = block_rows(g)
            m = jnp.max(s, axis=-1, keepdims=True)
            e = jnp.exp2(s - m)
            den = jnp.sum(e, axis=-1, keepdims=True)
            vv = jnp.concatenate([v_rows(group, prev), v_rows(group, rows)], axis=0)
            pv = _dot(e.astype(BF16), vv)
            acc_ref[rows, :] = jnp.where(low_half, pv[:n], pv[n:])
            cur = st_ref[rows, :]
            cur = jnp.where(is_m0, m[:n], jnp.where(is_m1, m[n:], cur))
            cur = jnp.where(is_d0, den[:n], jnp.where(is_d1, den[n:], cur))
            st_ref[rows, :] = cur

        own = list(own_items)
        side_start = n_blocks // 4 if own else 0

        def issue_own(upto_block):
            while own and own[0][0] < (upto_block + 1) * n:
                own.pop(0)[1]()

        issue_own(1)
        s_next = scores(0)
        for g in range(n_blocks):
            issue_own(4 * (g + 1) + 1)
            s_cur = s_next
            if g + 1 < n_blocks:
                s_next = scores(g + 1)
            finish(g, s_cur)
            if g >= side_start:
                k, span = g - side_start, n_blocks - side_start
                for item in side_items[k * len(side_items) // span:(k + 1) * len(side_items) // span]:
                    item()
        assert not own

    run_group(0, gather(0), [item for _, item in gather(1)])
    run_group(1, [], scatter(0))
    for item in scatter(1):
        item()

    @pl.when(step == pl.num_programs(1) - 1)
    def _():
        def sink(rows, vals):
            stats_ref[0, rows, :] = vals
        for item in _scatter_items(st_ref, stmp_ref, sink, dilation):
            item()


def _attention_pattern(proj3, slopes, *, dilation):
    bsz, seq, _ = proj3.shape
    width = ATTN_HEADS * ATTN_HEAD_DIM
    assert seq % (ATTN_WINDOW_STEPS * dilation) == 0 and seq % (4 * MOVE_ROWS) == 0
    kb = width // LANES

    def in_map(section, group):
        return lambda b, s, *_: (b, 0, section * kb + 2 * s + group)

    kernel = functools.partial(_attn_kernel, dilation=dilation)
    tile = lambda dtype: pltpu.VMEM((seq, LANES), dtype)
    small = pltpu.VMEM((8, LANES), F32)
    o, stats = pl.pallas_call(
        kernel,
        grid_spec=pltpu.PrefetchScalarGridSpec(
            num_scalar_prefetch=1,
            grid=(bsz, kb // 2),
            in_specs=[pl.BlockSpec((1, seq, LANES), in_map(section, group))
                      for group in range(2) for section in range(3)],
            out_specs=[
                pl.BlockSpec((1, seq, 2 * LANES), lambda b, s, *_: (b, 0, s)),
                pl.BlockSpec((1, seq, LANES), lambda b, s, *_: (b, 0, 0)),
            ],
            scratch_shapes=[
                pltpu.VMEM((2, 2, seq, LANES), BF16),
                pltpu.VMEM((2, seq, LANES), BF16),
                pltpu.VMEM((2, seq, LANES), BF16),
                *[tile(F32) if dilation > 1 else small for _ in range(3)],
                *[tile(F32) if dilation == 16 else small for _ in range(3)],
                tile(F32),
                tile(F32),
                tile(F32),
                tile(F32),
            ],
        ),
        out_shape=[
            jax.ShapeDtypeStruct((bsz, seq, width), BF16),
            jax.ShapeDtypeStruct((bsz, seq, LANES), F32),
        ],
        compiler_params=pltpu.CompilerParams(
            dimension_semantics=("parallel", "arbitrary"),
            vmem_limit_bytes=VMEM_LIMIT_BYTES),
        name=f"dilated_attn_d{dilation}",
    )(slopes, *([proj3] * 6))
    return o.reshape(bsz * seq, width), stats.reshape(bsz * seq, LANES)


def _split_bf16(x):
    hi = x.astype(BF16)
    return hi, (x - hi.astype(F32)).astype(BF16)


def _gla_kernel(q_ref, k_ref, v_ref, z_ref, code_ref, walpha_ref, balpha_ref, gain_ref,
                o_ref, b_ref, *, group):
    c = GLA_CHUNK
    seq, dk = q_ref.shape[1], q_ref.shape[2]
    dv = v_ref.shape[2]
    n_groups = seq // (c * group)

    ri = lax.broadcasted_iota(jnp.int32, (c, c), 0)
    ci = lax.broadcasted_iota(jnp.int32, (c, c), 1)
    causal = ri >= ci
    tri = causal.astype(BF16)
    gain = gain_ref[...]
    w_hi, w_lo = _split_bf16(walpha_ref[...])
    b_alpha = balpha_ref[...]

    def chunk_rows(g, u):
        return slice((g * group + u) * c, (g * group + u + 1) * c)

    def gate_logs(g):
        rows = slice(g * group * c, (g + 1) * group * c)
        code_hi, code_lo = _split_bf16(code_ref[0, rows, :])
        logits = _dot(code_hi, w_hi) + _dot(code_hi, w_lo) + _dot(code_lo, w_hi) + b_alpha
        log_sig = jnp.minimum(logits, 0.0) - jnp.log(1.0 + jnp.exp(-jnp.abs(logits)))
        b_ref[rows, :] = log_sig * (LOG2E / GLA_GATE_TAU)

    def cumsums(g):
        for u in range(group):
            rows = chunk_rows(g, u)
            la = b_ref[rows, :]
            la1 = la.astype(BF16)
            rem = la - la1.astype(F32)
            la2 = rem.astype(BF16)
            la3 = (rem - la2.astype(F32)).astype(BF16)
            b_ref[rows, :] = _dot(tri, la1) + _dot(tri, la2) + _dot(tri, la3)

    gate_logs(0)
    cumsums(0)
    st = jnp.zeros((dv, dk), F32)
    for g in range(n_groups):
        rows = [chunk_rows(g, u) for u in range(group)]
        q_dec, k_inv, k_end, decay, v = [], [], [], [], []
        for u in range(group):
            b = b_ref[rows[u], :]
            b_last = b[c - 1:c, :]
            q = q_ref[0, rows[u], :].astype(F32) * (dk ** -0.5)
            k = k_ref[0, rows[u], :].astype(F32)
            q_dec.append((q * jnp.exp2(b)).astype(BF16))
            k_undecayed = k * jnp.exp2(-b)
            decay.append(jnp.exp2(b_last))
            k_inv.append(k_undecayed.astype(BF16))
            k_end.append((k_undecayed * decay[u]).astype(BF16))
            v.append(v_ref[0, rows[u], :])
        a = [jnp.where(causal, _dot_nt(q_dec[u], k_inv[u]), 0.0).astype(BF16) for u in range(group)]
        d_state = [_dot_tn(v[u], k_end[u]) for u in range(group)]
        o_intra = [_dot(a[u], v[u]) for u in range(group)]
        if g + 1 < n_groups:
            gate_logs(g + 1)
        o = []
        for u in range(group):
            o.append(o_intra[u] + _dot_nt(q_dec[u], st.astype(BF16)))
            st = decay[u] * st + d_state[u]
        if g + 1 < n_groups:
            cumsums(g + 1)
        for u in range(group):
            mu = jnp.mean(o[u], axis=-1, keepdims=True)
            cen = o[u] - mu
            var = jnp.mean(cen * cen, axis=-1, keepdims=True)
            y = cen * lax.rsqrt(var + GROUP_NORM_EPS) * gain
            z = z_ref[0, rows[u], :].astype(F32)
            o_ref[0, rows[u], :] = (y * (z * _sigmoid(z))).astype(o_ref.dtype)


def _gla(proj3, code3, w_alpha, b_alpha, gain, *, q_col, k_col, v_col, z_col, group):
    bsz, seq, _ = proj3.shape
    dk = w_alpha.shape[1] // GLA_HEADS
    dv = gain.shape[1] // GLA_HEADS
    assert seq % (GLA_CHUNK * group) == 0
    kernel = functools.partial(_gla_kernel, group=group)
    return pl.pallas_call(
        kernel,
        grid=(bsz, GLA_HEADS),
        in_specs=[
            pl.BlockSpec((1, seq, dk), lambda b, h: (b, 0, q_col // dk + h)),
            pl.BlockSpec((1, seq, dk), lambda b, h: (b, 0, k_col // dk + h)),
            pl.BlockSpec((1, seq, dv), lambda b, h: (b, 0, v_col // dv + h)),
            pl.BlockSpec((1, seq, dv), lambda b, h: (b, 0, z_col // dv + h)),
            pl.BlockSpec((1, seq, LANES), lambda b, h: (b, 0, 0)),
            pl.BlockSpec((LANES, dk), lambda b, h: (0, h)),
            pl.BlockSpec((1, dk), lambda b, h: (0, h)),
            pl.BlockSpec((1, dv), lambda b, h: (0, h)),
        ],
        out_specs=pl.BlockSpec((1, seq, dv), lambda b, h: (b, 0, h)),
        out_shape=jax.ShapeDtypeStruct((bsz, seq, GLA_HEADS * dv), BF16),
        scratch_shapes=[pltpu.VMEM((seq, dk), F32)],
        compiler_params=pltpu.CompilerParams(
            dimension_semantics=("parallel", "parallel"),
            vmem_limit_bytes=VMEM_LIMIT_BYTES),
        name="gla",
    )(proj3, proj3, proj3, proj3, code3, w_alpha, b_alpha, gain)


def _out_kernel(a1_ref, a2_ref, a3_ref, s1_ref, s2_ref, s3_ref, za_ref, ob_ref, ga_ref, gb_ref, x_ref,
                expand_ref, wa_ref, wb_ref, wo_ref, bgate_ref, fgain_ref, out_ref, *, chunks):
    tm, d = x_ref.shape
    rc = tm // chunks
    expand = expand_ref[...]
    bg = bgate_ref[...]
    head_lane = lax.broadcasted_iota(jnp.int32, (rc, LANES), 1) < ATTN_HEADS
    ya, yb = [], []
    for c in range(chunks):
        rows = slice(c * rc, (c + 1) * rc)
        yb.append(_dot(ob_ref[rows, :], wb_ref[...]))
        stats = [s_ref[rows, :] for s_ref in (s1_ref, s2_ref, s3_ref)]
        m = jnp.maximum(jnp.maximum(stats[0], stats[1]), stats[2])
        scale = [jnp.exp2(s - m) for s in stats]
        den = [pltpu.roll(s, LANES - DEN_LANE, 1) for s in stats]
        inv = 1.0 / (scale[0] * den[0] + scale[1] * den[1] + scale[2] * den[2])
        oa = None
        for p, a_ref in enumerate((a1_ref, a2_ref, a3_ref)):
            weight = jnp.where(head_lane, scale[p] * inv, 0.0).astype(BF16)
            term = _dot(weight, expand) * a_ref[rows, :].astype(F32)
            oa = term if oa is None else oa + term
        za = za_ref[rows, :].astype(F32)
        ya.append(_dot((oa * (za * _sigmoid(za))).astype(BF16), wa_ref[...]))
    h = []
    for c in range(chunks):
        rows = slice(c * rc, (c + 1) * rc)
        gate_a = _sigmoid(ga_ref[rows, :].astype(F32) + bg[:, :d])
        gate_b = _sigmoid(gb_ref[rows, :].astype(F32) + bg[:, d:])
        merged = (gate_a * ya[c] + gate_b * yb[c]).astype(BF16)
        h.append(x_ref[rows, :] + _dot(merged, wo_ref[...]))
    for c in range(chunks):
        rows = slice(c * rc, (c + 1) * rc)
        ms = jnp.mean(h[c] * h[c], axis=-1, keepdims=True)
        out_ref[rows, :] = h[c] * lax.rsqrt(ms + NORM_EPS) * fgain_ref[...]


def _out_stage(a_pats, s_pats, proj2, ob2, x2, expand, wa, wb, wo, bgate, fgain, *,
               tm, chunks, za_col, ga_col, gb_col):
    t, d = x2.shape
    row_blk = lambda cb: pl.BlockSpec((tm, d), lambda i, cb=cb: (i, cb))
    stat_blk = pl.BlockSpec((tm, LANES), lambda i: (i, 0))
    full = lambda a: pl.BlockSpec(a.shape, lambda i: (0, 0))
    return pl.pallas_call(
        functools.partial(_out_kernel, chunks=chunks),
        grid=(t // tm,),
        in_specs=[row_blk(0), row_blk(0), row_blk(0), stat_blk, stat_blk, stat_blk,
                  row_blk(za_col // d), row_blk(0), row_blk(ga_col // d), row_blk(gb_col // d), row_blk(0),
                  full(expand), full(wa), full(wb), full(wo), full(bgate), full(fgain)],
        out_specs=row_blk(0),
        out_shape=jax.ShapeDtypeStruct((t, d), F32),
        compiler_params=pltpu.CompilerParams(
            dimension_semantics=("parallel",),
            vmem_limit_bytes=VMEM_LIMIT_BYTES),
        name="out_stage",
    )(*a_pats, *s_pats, proj2, ob2, proj2, proj2, x2, expand, wa, wb, wo, bgate, fgain)


def kernel(x, norm_gain, w_in, b_gate, w_alpha, b_alpha, gla_norm_gain,
           w_out_attn, w_out_gla, w_out, final_norm_gain):
    bsz, seq, d = x.shape
    assert w_in.shape[0] == 1, "single-layer block"
    aw = ATTN_HEADS * ATTN_HEAD_DIM
    gk = w_alpha.shape[-1]
    gv = gla_norm_gain.shape[-1]
    sizes = (aw, aw, aw, aw, gk, gk, gv, gv, GLA_GATE_RANK, d, d)
    offs = [0]
    for s in sizes:
        offs.append(offs[-1] + s)
    assert offs[-1] == w_in.shape[-1]
    code_lo, code_hi = offs[8], offs[9]

    w = w_in[0].astype(BF16)
    w_tail = w[:, code_hi:]
    w_code = jnp.pad(w[:, code_lo:code_hi], ((0, 0), (0, LANES - GLA_GATE_RANK)))
    w_alpha_p = jnp.pad(w_alpha[0], ((0, LANES - GLA_GATE_RANK), (0, 0)))
    slopes = 2.0 ** (-8.0 * (jnp.arange(ATTN_HEADS, dtype=F32) + 1.0) / ATTN_HEADS)
    head_of_col = jnp.arange(aw, dtype=jnp.int32) // ATTN_HEAD_DIM
    expand = (jnp.arange(LANES, dtype=jnp.int32)[:, None] == head_of_col[None, :]).astype(BF16)

    x2 = x.reshape(bsz * seq, d)
    proj, code = _in_proj(x2, norm_gain, w, code_lo, w_tail, w_code,
                          first_tile_scale=ATTN_HEAD_DIM ** -0.5 * LOG2E, tm=IN_PROJ_ROWS, tn=aw)
    proj3 = proj.reshape(bsz, seq, -1)
    rel = lambda o: o - (GLA_GATE_RANK if o > code_lo else 0)

    pats = [_attention_pattern(proj3, slopes, dilation=dil) for dil in DILATIONS]
    a_pats = [p[0] for p in pats]
    s_pats = [p[1] for p in pats]

    ob = _gla(proj3, code.reshape(bsz, seq, LANES), w_alpha_p, b_alpha, gla_norm_gain,
              q_col=rel(offs[4]), k_col=rel(offs[5]), v_col=rel(offs[6]), z_col=rel(offs[7]), group=GLA_GROUP)

    out = _out_stage(a_pats, s_pats, proj, ob.reshape(bsz * seq, gv), x2, expand,
                     w_out_attn[0].astype(BF16), w_out_gla[0].astype(BF16), w_out[0].astype(BF16),
                     b_gate, final_norm_gain.reshape(1, d),
                     tm=OUT_ROWS, chunks=OUT_CHUNKS, za_col=rel(offs[3]), ga_col=rel(offs[9]), gb_col=rel(offs[10]))
    return out.reshape(bsz, seq, d)
```
